```python
import math
import jax, jax.numpy as jnp
from jax import lax
import numpy as np

D_MODEL = 2048
BATCH = 8
SEQ = 4096
DEPTH = 1

GRID_W = 64
CTX_LEN = 256
CONV_WIDTH = D_MODEL // 2
CONV_TAPS = 3
N_DIFF_HEADS = 8
DIFF_HEAD_DIM = 64
ATTN_WIDTH = N_DIFF_HEADS * 2 * DIFF_HEAD_DIM
ROPE_THETA = 10000.0
AXIS_ROT = DIFF_HEAD_DIM // 2
Q_BLOCK = 128
RMS_EPS = 1e-6
QK_SCALE = DIFF_HEAD_DIM ** -0.5

A_OFF = 0
Q_OFF = A_OFF + 4 * CONV_WIDTH
K_OFF = Q_OFF + ATTN_WIDTH
V_OFF = K_OFF + ATTN_WIDTH
ZB_OFF = V_OFF + ATTN_WIDTH
G_OFF = ZB_OFF + ATTN_WIDTH
PROJ_COLS = G_OFF + 2 * D_MODEL

kernel_name = "hybrid_shortconv_diffattn_dit_block"


def _rms(x):
    xf = x.astype(jnp.float32)
    return (xf * lax.rsqrt(jnp.mean(xf * xf, axis=-1, keepdims=True) + RMS_EPS)).astype(x.dtype)


def _axial_rope_tables(n_tokens, dtype):
    rows = n_tokens // GRID_W
    row = jnp.repeat(jnp.arange(rows, dtype=jnp.float32), GRID_W)
    col = jnp.tile(jnp.arange(GRID_W, dtype=jnp.float32), rows)
    inv_freq = ROPE_THETA ** (-jnp.arange(0, AXIS_ROT, 2, dtype=jnp.float32) / AXIS_ROT)
    ang_r = row[:, None] * inv_freq
    ang_c = col[:, None] * inv_freq
    ang = jnp.concatenate([ang_r, ang_r, ang_c, ang_c], axis=-1)
    return jnp.cos(ang).astype(dtype), jnp.sin(ang).astype(dtype)


def _rotate_half(y):
    y1, y2 = jnp.split(y, 2, axis=-1)
    return jnp.concatenate([-y2, y1], axis=-1)


def _apply_axial_rope(t, cos, sin):
    tr, tc = jnp.split(t, 2, axis=-1)
    rot = jnp.concatenate([_rotate_half(tr), _rotate_half(tc)], axis=-1)
    return t * cos[:, None, None, :] + rot * sin[:, None, None, :]


def _heads_qk(t):
    return t.reshape(t.shape[0], t.shape[1], N_DIFF_HEADS, 2, DIFF_HEAD_DIM)


def _heads_v(t):
    return t.reshape(t.shape[0], t.shape[1], N_DIFF_HEADS, 2 * DIFF_HEAD_DIM)


def _diff_softmax_mix(q, k, v, lam):
    s = jnp.einsum("bqhcd,bkhcd->bchqk", q, k).astype(jnp.float32)
    p = jax.nn.softmax(s, axis=-1)
    a = p[:, 0] - lam * p[:, 1]
    return jnp.einsum("bhqk,bkhe->bqhe", a.astype(v.dtype), v)


def _latent_diff_attention(q, k_lat, v_lat, k_ctx, v_ctx, lam):
    b, s = q.shape[0], q.shape[1]
    k = jnp.concatenate([k_lat, k_ctx], axis=1)
    v = jnp.concatenate([v_lat, v_ctx], axis=1)
    n_blk = s // Q_BLOCK
    q_blocks = q.reshape(b, n_blk, Q_BLOCK, N_DIFF_HEADS, 2, DIFF_HEAD_DIM).transpose(1, 0, 2, 3, 4, 5)
    out = lax.map(lambda qb: _diff_softmax_mix(qb, k, v, lam), q_blocks)
    return out.transpose(1, 0, 2, 3, 4).reshape(b, s, N_DIFF_HEADS, 2 * DIFF_HEAD_DIM)


def _short_conv_branch(p_a, conv_w, w_out_a):
    xa, gb, gc, za = jnp.split(p_a, 4, axis=-1)
    u = gc * xa
    up = jnp.pad(u, ((0, 0), (1, 1), (0, 0)))
    y = conv_w[0] * up[:, :-2] + conv_w[1] * up[:, 1:-1] + conv_w[2] * up[:, 2:]
    return (gb * y * jax.nn.silu(za)) @ w_out_a


def _diff_attn_out(o, zb, subln_w, lam_init, w_out_b):
    o = _rms(o) * subln_w * (1.0 - lam_init)
    o = o.reshape(o.shape[0], o.shape[1], ATTN_WIDTH)
    return (o * jax.nn.silu(zb)) @ w_out_b


def _merge(y_a, y_b, p_g, w_o):
    g_a, g_b = jnp.split(jax.nn.sigmoid(p_g), 2, axis=-1)
    return (g_a * y_a + g_b * y_b) @ w_o


def setup_inputs(seed: int = 0) -> dict:
    key = jax.random.key(seed)
    ks = jax.random.split(key, 14)
    f32 = jnp.float32
    x = jax.random.normal(ks[0], (BATCH, SEQ, D_MODEL), f32)
    c = jax.random.normal(ks[1], (BATCH, D_MODEL), f32)
    ctx = jax.random.normal(ks[2], (BATCH, CTX_LEN, D_MODEL), f32)
    c_ctx = jax.random.normal(ks[3], (D_MODEL,), f32)
    w_ada = jax.random.normal(ks[4], (DEPTH, D_MODEL, 3 * D_MODEL), f32) * (0.5 * D_MODEL ** -0.5)
    b_ada = jax.random.normal(ks[5], (DEPTH, 3 * D_MODEL), f32) * 0.01
    w_in = jax.random.normal(ks[6], (DEPTH, D_MODEL, PROJ_COLS), f32) * (D_MODEL ** -0.5)
    conv_w = jax.random.normal(ks[7], (DEPTH, CONV_TAPS, CONV_WIDTH), f32) * (CONV_TAPS ** -0.5)
    diff_lambda = jax.random.normal(ks[8], (DEPTH, 4, DIFF_HEAD_DIM), f32) * 0.1
    subln_w = 1.0 + 0.01 * jax.random.normal(ks[9], (DEPTH, 2 * DIFF_HEAD_DIM), f32)
    w_out_a = jax.random.normal(ks[10], (DEPTH, CONV_WIDTH, D_MODEL), f32) * (CONV_WIDTH ** -0.5)
    w_out_b = jax.random.normal(ks[11], (DEPTH, ATTN_WIDTH, D_MODEL), f32) * (ATTN_WIDTH ** -0.5)
    w_o = jax.random.normal(ks[12], (DEPTH, D_MODEL, D_MODEL), f32) * (D_MODEL ** -0.5)
    final_norm_w = 1.0 + 0.01 * jax.random.normal(ks[13], (D_MODEL,), f32)
    return {"x": x, "c": c, "ctx": ctx, "c_ctx": c_ctx, "w_ada": w_ada, "b_ada": b_ada,
            "w_in": w_in, "conv_w": conv_w, "diff_lambda": diff_lambda, "subln_w": subln_w,
            "w_out_a": w_out_a, "w_out_b": w_out_b, "w_o": w_o, "final_norm_w": final_norm_w}


def reference(x, c, ctx, c_ctx, w_ada, b_ada, w_in, conv_w, diff_lambda, subln_w,
              w_out_a, w_out_b, w_o, final_norm_w):
    cos, sin = _axial_rope_tables(x.shape[1], x.dtype)
    for l in range(DEPTH):
        lam_init = 0.8 - 0.6 * math.exp(-0.3 * l)
        dl = diff_lambda[l].astype(jnp.float32)
        lam = jnp.exp(jnp.sum(dl[0] * dl[1])) - jnp.exp(jnp.sum(dl[2] * dl[3])) + lam_init

        shift, scale, gate = jnp.split(jax.nn.silu(c) @ w_ada[l] + b_ada[l], 3, axis=-1)
        c_shift, c_scale, c_gate = jnp.split(jax.nn.silu(c_ctx) @ w_ada[l] + b_ada[l], 3, axis=-1)
        h = _rms(x) * (1.0 + scale[:, None]) + shift[:, None]
        hc = _rms(ctx) * (1.0 + c_scale) + c_shift

        if l + 1 < DEPTH:
            pc = hc @ w_in[l]
            q_c = _heads_qk(pc[..., Q_OFF:K_OFF]) * QK_SCALE
            k_c = _heads_qk(pc[..., K_OFF:V_OFF])
            v_c = _heads_v(pc[..., V_OFF:ZB_OFF])
            o_c = _diff_softmax_mix(q_c, k_c, v_c, lam)
            y_ac = _short_conv_branch(pc[..., A_OFF:Q_OFF], conv_w[l], w_out_a[l])
            y_bc = _diff_attn_out(o_c, pc[..., ZB_OFF:G_OFF], subln_w[l], lam_init, w_out_b[l])
            ctx_next = ctx + c_gate * _merge(y_ac, y_bc, pc[..., G_OFF:], w_o[l])
        else:
            pkv = hc @ w_in[l][:, K_OFF:ZB_OFF]
            k_c = _heads_qk(pkv[..., :ATTN_WIDTH])
            v_c = _heads_v(pkv[..., ATTN_WIDTH:])
            ctx_next = ctx

        p = h @ w_in[l]
        q = _apply_axial_rope(_heads_qk(p[..., Q_OFF:K_OFF]), cos, sin) * QK_SCALE
        k = _apply_axial_rope(_heads_qk(p[..., K_OFF:V_OFF]), cos, sin)
        v = _heads_v(p[..., V_OFF:ZB_OFF])
        o = _latent_diff_attention(q, k, v, k_c, v_c, lam)
        y_a = _short_conv_branch(p[..., A_OFF:Q_OFF], conv_w[l], w_out_a[l])
        y_b = _diff_attn_out(o, p[..., ZB_OFF:G_OFF], subln_w[l], lam_init, w_out_b[l])
        x = x + gate[:, None] * _merge(y_a, y_b, p[..., G_OFF:], w_o[l])
        ctx = ctx_next
    return _rms(x) * final_norm_w
```

```python
import functools
import math

import jax
import jax.numpy as jnp
from jax import lax
from jax.experimental import pallas as pl
from jax.experimental.pallas import tpu as pltpu

F32 = jnp.float32
BF16 = jnp.bfloat16

N_HEADS = 8
HEAD_DIM = 64
HEAD_W = 2 * HEAD_DIM
GRID_W = 64
ROPE_THETA = 10000.0
AXIS_ROT = HEAD_DIM // 2
RMS_EPS = 1e-6
QK_SCALE = HEAD_DIM ** -0.5
LOG2E = math.log2(math.e)
LANES = 128
COL_TILE = 1024

ROW_TILE = 512
Q_TILE = 256
KEY_CHUNK = 512
OUT_TILE = 256
VMEM_LIMIT = 56 * 1024 * 1024


def _silu(x):
    return x * jax.nn.sigmoid(x)


def _rms_mod(x, shift, scale):
    ms = jnp.mean(x * x, axis=-1, keepdims=True)
    return (x * lax.rsqrt(ms + RMS_EPS)) * (1.0 + scale) + shift


def _mod_kernel(c_ref, w_ref, b_ref, o_ref):
    a = _silu(c_ref[...]).astype(BF16)
    o_ref[...] = jnp.dot(a, w_ref[...].astype(BF16), preferred_element_type=F32) + b_ref[...]


def _modulation(cc, w_ada, b_ada):
    n, d = cc.shape
    cols = w_ada.shape[1]
    return pl.pallas_call(
        _mod_kernel,
        grid=(cols // COL_TILE,),
        in_specs=[
            pl.BlockSpec((n, d), lambda j: (0, 0)),
            pl.BlockSpec((d, COL_TILE), lambda j: (0, j)),
            pl.BlockSpec((1, COL_TILE), lambda j: (0, j)),
        ],
        out_specs=pl.BlockSpec((n, COL_TILE), lambda j: (0, j)),
        out_shape=jax.ShapeDtypeStruct((n, cols), F32),
        compiler_params=pltpu.CompilerParams(
            dimension_semantics=("parallel",), vmem_limit_bytes=VMEM_LIMIT),
        name="adaln_mod",
    )(cc, w_ada, b_ada.reshape(1, cols))


def _store_heads(ref, t):
    for hh in range(N_HEADS):
        ref[0, hh] = t[:, hh * HEAD_W:(hh + 1) * HEAD_W].astype(ref.dtype)


def _rope_heads(ref, t, cos, sin, scale):
    for hh in range(N_HEADS):
        th = t[:, hh * HEAD_W:(hh + 1) * HEAD_W]
        rot = pltpu.roll(th, HEAD_W // 2, 1)
        ref[0, hh] = ((th * cos + rot * sin) * scale).astype(ref.dtype)


def _norm_to_scratch(x_ref, mod_ref, h_ref):
    shift = mod_ref[0, 0:1, :]
    scale = mod_ref[0, 1:2, :]
    h_ref[...] = _rms_mod(x_ref[0], shift, scale).astype(BF16)


def _inproj_kernel(x_ref, mod_ref, w_ref, cos_ref, sin_ref,
                   pe_ref, q_ref, k_ref, v_ref, h_ref, *, n_pe):
    j = pl.program_id(2)

    @pl.when(j == 0)
    def _():
        _norm_to_scratch(x_ref, mod_ref, h_ref)

    def proj():
        return jnp.dot(h_ref[...], w_ref[...], preferred_element_type=F32)

    @pl.when(j < n_pe)
    def _():
        pe_ref[0] = proj()

    @pl.when(j == n_pe)
    def _():
        _rope_heads(q_ref, proj(), cos_ref[...], sin_ref[...], QK_SCALE * LOG2E)

    @pl.when(j == n_pe + 1)
    def _():
        _rope_heads(k_ref, proj(), cos_ref[...], sin_ref[...], 1.0)

    @pl.when(j == n_pe + 2)
    def _():
        _store_heads(v_ref, proj())


def _inproj(x, mod, w, cos_t, sin_t, n_pe):
    b, s, d = x.shape
    tm = min(ROW_TILE, s)
    nj = n_pe + 3
    head_shape = jax.ShapeDtypeStruct((b, N_HEADS, s, HEAD_W), BF16)
    head_spec = pl.BlockSpec((1, N_HEADS, tm, HEAD_W), lambda bi, i, j: (bi, 0, i, 0))
    return pl.pallas_call(
        functools.partial(_inproj_kernel, n_pe=n_pe),
        grid=(b, s // tm, nj),
        in_specs=[
            pl.BlockSpec((1, tm, d), lambda bi, i, j: (bi, i, 0)),
            pl.BlockSpec((1, 3, d), lambda bi, i, j: (bi, 0, 0)),
            pl.BlockSpec((d, COL_TILE), lambda bi, i, j: (0, j)),
            pl.BlockSpec((tm, HEAD_W), lambda bi, i, j: (i, 0)),
            pl.BlockSpec((tm, HEAD_W), lambda bi, i, j: (i, 0)),
        ],
        out_specs=[
            pl.BlockSpec((1, tm, COL_TILE), lambda bi, i, j: (bi, i, jnp.minimum(j, n_pe - 1))),
            head_spec, head_spec, head_spec,
        ],
        out_shape=[
            jax.ShapeDtypeStruct((b, s, n_pe * COL_TILE), F32),
            head_shape, head_shape, head_shape,
        ],
        scratch_shapes=[pltpu.VMEM((tm, d), BF16)],
        compiler_params=pltpu.CompilerParams(
            dimension_semantics=("parallel", "parallel", "arbitrary"),
            vmem_limit_bytes=VMEM_LIMIT),
        name="in_proj",
    )(x, mod, w, cos_t, sin_t)


def _ctx_kernel(x_ref, mod_ref, w_ref, k_ref, v_ref, h_ref):
    j = pl.program_id(1)

    @pl.when(j == 0)
    def _():
        _norm_to_scratch(x_ref, mod_ref, h_ref)

    def proj():
        return jnp.dot(h_ref[...], w_ref[...], preferred_element_type=F32)

    @pl.when(j == 0)
    def _():
        _store_heads(k_ref, proj())

    @pl.when(j == 1)
    def _():
        _store_heads(v_ref, proj())


def _ctx_proj(ctx, mod, w, k_col):
    b, c, d = ctx.shape
    ctx_row = mod.shape[0] - 1
    head_shape = jax.ShapeDtypeStruct((b, N_HEADS, c, HEAD_W), BF16)
    head_spec = pl.BlockSpec((1, N_HEADS, c, HEAD_W), lambda bi, j: (bi, 0, 0, 0))
    return pl.pallas_call(
        _ctx_kernel,
        grid=(b, 2),
        in_specs=[
            pl.BlockSpec((1, c, d), lambda bi, j: (bi, 0, 0)),
            pl.BlockSpec((1, 3, d), lambda bi, j: (ctx_row, 0, 0)),
            pl.BlockSpec((d, COL_TILE), lambda bi, j: (0, k_col + j)),
        ],
        out_specs=[head_spec, head_spec],
        out_shape=[head_shape, head_shape],
        scratch_shapes=[pltpu.VMEM((c, d), BF16)],
        compiler_params=pltpu.CompilerParams(
            dimension_semantics=("parallel", "arbitrary"), vmem_limit_bytes=VMEM_LIMIT),
        name="ctx_proj",
    )(ctx, mod, w)


def _lane_blocks_max(s):
    m = s[:, 0:LANES]
    for u in range(1, s.shape[1] // LANES):
        m = jnp.maximum(m, s[:, u * LANES:(u + 1) * LANES])
    return m


def _exp2_shifted(s, mb):
    blocks = [jnp.exp2(s[:, u * LANES:(u + 1) * LANES] - mb) for u in range(s.shape[1] // LANES)]
    return jnp.concatenate(blocks, axis=1).astype(BF16)


def _nt_dot(a, b):
    return lax.dot_general(a, b, (((1,), (1,)), ((), ())), preferred_element_type=F32)


def _attn_kernel(dl_ref, sub_ref, q_ref, kl_ref, kc_ref, vl_ref, vc_ref, o_ref,
                 vx_ref, lhs_ref, s_ref, sc_ref, m_ref, acc_ref, *, tq, ck, lam_init):
    s_len = q_ref.shape[2]
    c_len = kc_ref.shape[2]
    n_chunks = s_len // ck

    vx_ref[0:s_len, 0:HEAD_W] = vl_ref[0, 0]
    vx_ref[s_len:s_len + c_len, 0:HEAD_W] = vc_ref[0, 0]
    vx_ref[:, HEAD_W:2 * HEAD_W] = jnp.ones((s_len + c_len, HEAD_W), BF16)

    dl = dl_ref[...]
    lam = (jnp.exp(jnp.sum(dl[0:1] * dl[1:2], axis=1, keepdims=True))
           - jnp.exp(jnp.sum(dl[2:3] * dl[3:4], axis=1, keepdims=True)) + lam_init)
    sub = sub_ref[...] * (1.0 - lam_init)
    lane = lax.broadcasted_iota(jnp.int32, (1, HEAD_W), 1)
    comp0 = (lane % HEAD_DIM) < (HEAD_DIM // 2)

    def q_tile(t, carry):
        r0 = pl.multiple_of(t * tq, tq)
        q = q_ref[0, 0, pl.ds(r0, tq), :]
        zero = jnp.zeros_like(q)
        lhs_ref[0:tq, :] = jnp.where(comp0, q, zero)
        lhs_ref[tq:2 * tq, :] = jnp.where(comp0, zero, q)

        m_ref[...] = jnp.full(m_ref.shape, -jnp.inf, F32)

        def scores(c, carry2):
            k0 = pl.multiple_of(c * ck, ck)
            s = _nt_dot(lhs_ref[...], kl_ref[0, 0, pl.ds(k0, ck), :])
            s_ref[c] = s
            m_ref[...] = jnp.maximum(m_ref[...], _lane_blocks_max(s))
            return carry2

        lax.fori_loop(0, n_chunks, scores, 0)
        sc = _nt_dot(lhs_ref[...], kc_ref[0, 0])
        sc_ref[...] = sc
        mrow = jnp.max(jnp.maximum(m_ref[...], _lane_blocks_max(sc)), axis=1, keepdims=True)
        m_ref[...] = jnp.broadcast_to(mrow, m_ref.shape)

        acc_ref[...] = jnp.zeros(acc_ref.shape, F32)

        def weighted(c, carry2):
            k0 = pl.multiple_of(c * ck, ck)
            p = _exp2_shifted(s_ref[c], m_ref[...])
            acc_ref[...] += jnp.dot(p, vx_ref[pl.ds(k0, ck), :], preferred_element_type=F32)
            return carry2

        lax.fori_loop(0, n_chunks, weighted, 0)
        pc = _exp2_shifted(sc_ref[...], m_ref[...])
        acc = acc_ref[...] + jnp.dot(pc, vx_ref[s_len:s_len + c_len, :],
                                     preferred_element_type=F32)

        o0 = acc[0:tq, 0:HEAD_W] / acc[0:tq, HEAD_W:2 * HEAD_W]
        o1 = acc[tq:2 * tq, 0:HEAD_W] / acc[tq:2 * tq, HEAD_W:2 * HEAD_W]
        o = o0 - lam * o1
        o = o * lax.rsqrt(jnp.mean(o * o, axis=-1, keepdims=True) + RMS_EPS) * sub
        o_ref[0, pl.ds(r0, tq), :] = o
        return carry

    lax.fori_loop(0, s_len // tq, q_tile, 0)


def _attention(dl, subln, q, kl, kc, vl, vc, lam_init):
    b, h, s, _ = q.shape
    c = kc.shape[2]
    tq = min(Q_TILE, s)
    ck = min(KEY_CHUNK, s)
    lat_spec = pl.BlockSpec((1, 1, s, HEAD_W), lambda bi, hi: (bi, hi, 0, 0))
    ctx_spec = pl.BlockSpec((1, 1, c, HEAD_W), lambda bi, hi: (bi, hi, 0, 0))
    return pl.pallas_call(
        functools.partial(_attn_kernel, tq=tq, ck=ck, lam_init=lam_init),
        grid=(b, h),
        in_specs=[
            pl.BlockSpec(dl.shape, lambda bi, hi: (0, 0)),
            pl.BlockSpec(subln.shape, lambda bi, hi: (0, 0)),
            lat_spec, lat_spec, ctx_spec, lat_spec, ctx_spec,
        ],
        out_specs=pl.BlockSpec((1, s, HEAD_W), lambda bi, hi: (bi, 0, hi)),
        out_shape=jax.ShapeDtypeStruct((b, s, h * HEAD_W), F32),
        scratch_shapes=[
            pltpu.VMEM((s + c, 2 * HEAD_W), BF16),
            pltpu.VMEM((2 * tq, HEAD_W), BF16),
            pltpu.VMEM((s // ck, 2 * tq, ck), F32),
            pltpu.VMEM((2 * tq, c), F32),
            pltpu.VMEM((2 * tq, LANES), F32),
            pltpu.VMEM((2 * tq, 2 * HEAD_W), F32),
        ],
        compiler_params=pltpu.CompilerParams(
            dimension_semantics=("parallel", "parallel"), vmem_limit_bytes=VMEM_LIMIT),
        name="diff_attn",
    )(dl, subln, q, kl, kc, vl, vc)


def _out_kernel(xa_ref, gb_ref, gc_ref, za_ref, xap_ref, gcp_ref, xan_ref, gcn_ref,
                g_ref, zb_ref, o_ref, x_ref, mod_ref, cw_ref, woa_ref, wob_ref, wo_ref, fw_ref,
                out_ref, *, tm):
    i = pl.program_id(1)
    last = pl.num_programs(1) - 1
    d = x_ref.shape[2]

    u = gc_ref[0] * xa_ref[0]
    u_prev = jnp.where(i > 0, gcp_ref[0, 7:8, :] * xap_ref[0, 7:8, :], 0.0)
    u_next = jnp.where(i < last, gcn_ref[0, 0:1, :] * xan_ref[0, 0:1, :], 0.0)
    row = lax.broadcasted_iota(jnp.int32, (tm, 1), 0)
    u_dn = jnp.where(row == 0, u_prev, pltpu.roll(u, 1, 0))
    u_up = jnp.where(row == tm - 1, u_next, pltpu.roll(u, tm - 1, 0))
    y = cw_ref[0:1, :] * u_dn + cw_ref[1:2, :] * u + cw_ref[2:3, :] * u_up
    a_in = (gb_ref[0] * y * _silu(za_ref[0])).astype(BF16)
    y_a = jnp.dot(a_in, woa_ref[...], preferred_element_type=F32)

    b_in = (o_ref[0] * _silu(zb_ref[0])).astype(BF16)
    y_b = jnp.dot(b_in, wob_ref[...], preferred_element_type=F32)

    g = jax.nn.sigmoid(g_ref[0])
    merged = (g[:, 0:d] * y_a + g[:, d:2 * d] * y_b).astype(BF16)
    y_o = jnp.dot(merged, wo_ref[...], preferred_element_type=F32)

    xo = x_ref[0] + mod_ref[0, 2:3, :] * y_o
    out_ref[0] = xo * lax.rsqrt(jnp.mean(xo * xo, axis=-1, keepdims=True) + RMS_EPS) * fw_ref[...]


def _out_stage(pe, o, x, mod, conv_w, woa, wob, wo, fw):
    b, s, d = x.shape
    tm = min(OUT_TILE, s)
    t8 = tm // 8
    nblk8 = s // 8
    col = lambda cb: (lambda bi, i: (bi, i, cb))
    prev = lambda cb: (lambda bi, i: (bi, jnp.maximum(i * t8 - 1, 0), cb))
    nxt = lambda cb: (lambda bi, i: (bi, jnp.minimum((i + 1) * t8, nblk8 - 1), cb))
    const2 = lambda bi, i: (0, 0)
    resident = dict(pipeline_mode=pl.Buffered(1))
    return pl.pallas_call(
        functools.partial(_out_kernel, tm=tm),
        grid=(b, s // tm),
        in_specs=[
            pl.BlockSpec((1, tm, COL_TILE), col(0)),
            pl.BlockSpec((1, tm, COL_TILE), col(1)),
            pl.BlockSpec((1, tm, COL_TILE), col(2)),
            pl.BlockSpec((1, tm, COL_TILE), col(3)),
            pl.BlockSpec((1, 8, COL_TILE), prev(0)),
            pl.BlockSpec((1, 8, COL_TILE), prev(2)),
            pl.BlockSpec((1, 8, COL_TILE), nxt(0)),
            pl.BlockSpec((1, 8, COL_TILE), nxt(2)),
            pl.BlockSpec((1, tm, 2 * d), col(1)),
            pl.BlockSpec((1, tm, COL_TILE), col(8)),
            pl.BlockSpec((1, tm, COL_TILE), col(0)),
            pl.BlockSpec((1, tm, d), col(0)),
            pl.BlockSpec((1, 3, d), lambda bi, i: (bi, 0, 0)),
            pl.BlockSpec(conv_w.shape, const2),
            pl.BlockSpec(woa.shape, const2, **resident),
            pl.BlockSpec(wob.shape, const2, **resident),
            pl.BlockSpec(wo.shape, const2, **resident),
            pl.BlockSpec(fw.shape, const2),
        ],
        out_specs=pl.BlockSpec((1, tm, d), col(0)),
        out_shape=jax.ShapeDtypeStruct((b, s, d), x.dtype),
        compiler_params=pltpu.CompilerParams(
            dimension_semantics=("parallel", "parallel"), vmem_limit_bytes=VMEM_LIMIT),
        name="out_stage",
    )(pe, pe, pe, pe, pe, pe, pe, pe, pe, pe, o, x, mod, conv_w, woa, wob, wo, fw)


def _head_lane_order(w):
    lead = w.shape[:-1]
    w = w.reshape(lead + (N_HEADS, 2, 2, 2, AXIS_ROT // 2))
    n = len(lead)
    w = jnp.transpose(w, tuple(range(n)) + (n, n + 3, n + 1, n + 2, n + 4))
    return w.reshape(lead + (N_HEADS * HEAD_W,))


def _rope_tables(n_tokens):
    rows = n_tokens // GRID_W
    row = jnp.repeat(jnp.arange(rows, dtype=F32), GRID_W)
    col = jnp.tile(jnp.arange(GRID_W, dtype=F32), rows)
    inv_freq = ROPE_THETA ** (-jnp.arange(0, AXIS_ROT, 2, dtype=F32) / AXIS_ROT)
    ang = jnp.concatenate([row[:, None] * inv_freq, col[:, None] * inv_freq], axis=-1)
    ang = jnp.tile(ang, (1, 4))
    sign = jnp.where(jnp.arange(HEAD_W) < HEAD_W // 2, -1.0, 1.0).astype(F32)
    return jnp.cos(ang), jnp.sin(ang) * sign


def kernel(x, c, ctx, c_ctx, w_ada, b_ada, w_in, conv_w, diff_lambda, subln_w,
           w_out_a, w_out_b, w_o, final_norm_w):
    b, s, d = x.shape
    cw = conv_w.shape[-1]
    aw = w_out_b.shape[1]
    assert w_in.shape[0] == 1, "single-layer block"
    assert cw == COL_TILE and aw == COL_TILE == N_HEADS * HEAD_W and d == 2 * COL_TILE
    assert s % GRID_W == 0 and s % min(ROW_TILE, s) == 0 and ctx.shape[1] % LANES == 0
    lam_init = 0.8 - 0.6 * math.exp(-0.3 * 0)

    wi = w_in[0]
    a_end = 4 * cw
    wq, wk, wv, wzb = (wi[:, a_end + n * aw:a_end + (n + 1) * aw] for n in range(4))
    wg = wi[:, a_end + 4 * aw:]
    w = jnp.concatenate(
        [wi[:, :a_end], wg, wzb, _head_lane_order(wq), _head_lane_order(wk), wv],
        axis=1).astype(BF16)
    n_pe = (a_end + 2 * d + aw) // COL_TILE

    cc = jnp.concatenate([c, c_ctx[None, :]], axis=0)
    mod = _modulation(cc, w_ada[0], b_ada[0]).reshape(b + 1, 3, d)

    cos_t, sin_t = _rope_tables(s)
    pe, q, kl, vl = _inproj(x, mod, w, cos_t, sin_t, n_pe)
    kc, vc = _ctx_proj(ctx, mod, w, n_pe + 1)

    o = _attention(diff_lambda[0], subln_w[0].reshape(1, HEAD_W), q, kl, kc, vl, vc, lam_init)

    return _out_stage(pe, o, x, mod, conv_w[0],
                      w_out_a[0].astype(BF16), w_out_b[0].astype(BF16), w_o[0].astype(BF16),
                      final_norm_w.reshape(1, d))
```

```python
import functools
import math

import jax
import jax.numpy as jnp
from jax import lax
from jax.experimental import pallas as pl
from jax.experimental.pallas import tpu as pltpu

F32 = jnp.float32
BF16 = jnp.bfloat16

N_HEADS = 8
HEAD_DIM = 64
HEAD_W = 2 * HEAD_DIM
GRID_W = 64
ROPE_THETA = 10000.0
AXIS_ROT = HEAD_DIM // 2
RMS_EPS = 1e-6
QK_SCALE = HEAD_DIM ** -0.5
LOG2E = math.log2(math.e)
LANES = 128
COL_TILE = 1024

ROW_TILE = 512
Q_TILE = 256
KEY_CHUNK = 512
OUT_TILE = 256
VMEM_LIMIT = 56 * 1024 * 1024


def _silu(x):
    return x * jax.nn.sigmoid(x)


def _rms_mod(x, shift, scale):
    ms = jnp.mean(x * x, axis=-1, keepdims=True)
    return (x * lax.rsqrt(ms + RMS_EPS)) * (1.0 + scale) + shift


def _mod_kernel(c_ref, w_ref, b_ref, o_ref):
    a = _silu(c_ref[...]).astype(BF16)
    o_ref[...] = jnp.dot(a, w_ref[...].astype(BF16), preferred_element_type=F32) + b_ref[...]


def _modulation(cc, w_ada, b_ada):
    n, d = cc.shape
    cols = w_ada.shape[1]
    return pl.pallas_call(
        _mod_kernel,
        grid=(cols // COL_TILE,),
        in_specs=[
            pl.BlockSpec((n, d), lambda j: (0, 0)),
            pl.BlockSpec((d, COL_TILE), lambda j: (0, j)),
            pl.BlockSpec((1, COL_TILE), lambda j: (0, j)),
        ],
        out_specs=pl.BlockSpec((n, COL_TILE), lambda j: (0, j)),
        out_shape=jax.ShapeDtypeStruct((n, cols), F32),
        compiler_params=pltpu.CompilerParams(
            dimension_semantics=("parallel",), vmem_limit_bytes=VMEM_LIMIT),
        name="adaln_mod",
    )(cc, w_ada, b_ada.reshape(1, cols))


def _store_heads(ref, t):
    for hh in range(N_HEADS):
        ref[0, hh] = t[:, hh * HEAD_W:(hh + 1) * HEAD_W].astype(ref.dtype)


def _rope_heads(ref, t, cos, sin, scale):
    for hh in range(N_HEADS):
        th = t[:, hh * HEAD_W:(hh + 1) * HEAD_W]
        rot = pltpu.roll(th, HEAD_W // 2, 1)
        ref[0, hh] = ((th * cos + rot * sin) * scale).astype(ref.dtype)


def _norm_to_scratch(x_ref, mod_ref, h_ref):
    shift = mod_ref[0, 0:1, :]
    scale = mod_ref[0, 1:2, :]
    h_ref[...] = _rms_mod(x_ref[0], shift, scale).astype(BF16)


def _inproj_kernel(x_ref, mod_ref, w_ref, cos_ref, sin_ref,
                   pe_ref, q_ref, k_ref, v_ref, h_ref, *, n_pe):
    j = pl.program_id(2)

    @pl.when(j == 0)
    def _():
        _norm_to_scratch(x_ref, mod_ref, h_ref)

    def proj():
        return jnp.dot(h_ref[...], w_ref[...], preferred_element_type=F32)

    @pl.when(j < n_pe)
    def _():
        pe_ref[0] = proj()

    @pl.when(j == n_pe)
    def _():
        _rope_heads(q_ref, proj(), cos_ref[...], sin_ref[...], QK_SCALE * LOG2E)

    @pl.when(j == n_pe + 1)
    def _():
        _rope_heads(k_ref, proj(), cos_ref[...], sin_ref[...], 1.0)

    @pl.when(j == n_pe + 2)
    def _():
        _store_heads(v_ref, proj())


def _inproj(x, mod, w, cos_t, sin_t, n_pe):
    b, s, d = x.shape
    tm = min(ROW_TILE, s)
    nj = n_pe + 3
    head_shape = jax.ShapeDtypeStruct((b, N_HEADS, s, HEAD_W), BF16)
    head_spec = pl.BlockSpec((1, N_HEADS, tm, HEAD_W), lambda bi, i, j: (bi, 0, i, 0))
    return pl.pallas_call(
        functools.partial(_inproj_kernel, n_pe=n_pe),
        grid=(b, s // tm, nj),
        in_specs=[
            pl.BlockSpec((1, tm, d), lambda bi, i, j: (bi, i, 0)),
            pl.BlockSpec((1, 3, d), lambda bi, i, j: (bi, 0, 0)),
            pl.BlockSpec((d, COL_TILE), lambda bi, i, j: (0, j)),
            pl.BlockSpec((tm, HEAD_W), lambda bi, i, j: (i, 0)),
            pl.BlockSpec((tm, HEAD_W), lambda bi, i, j: (i, 0)),
        ],
        out_specs=[
            pl.BlockSpec((1, tm, COL_TILE), lambda bi, i, j: (bi, i, jnp.minimum(j, n_pe - 1))),
            head_spec, head_spec, head_spec,
        ],
        out_shape=[
            jax.ShapeDtypeStruct((b, s, n_pe * COL_TILE), F32),
            head_shape, head_shape, head_shape,
        ],
        scratch_shapes=[pltpu.VMEM((tm, d), BF16)],
        compiler_params=pltpu.CompilerParams(
            dimension_semantics=("parallel", "parallel", "arbitrary"),
            vmem_limit_bytes=VMEM_LIMIT),
        name="in_proj",
    )(x, mod, w, cos_t, sin_t)


def _ctx_kernel(x_ref, mod_ref, w_ref, k_ref, v_ref, h_ref):
    j = pl.program_id(1)

    @pl.when(j == 0)
    def _():
        _norm_to_scratch(x_ref, mod_ref, h_ref)

    def proj():
        return jnp.dot(h_ref[...], w_ref[...], preferred_element_type=F32)

    @pl.when(j == 0)
    def _():
        _store_heads(k_ref, proj())

    @pl.when(j == 1)
    def _():
        _store_heads(v_ref, proj())


def _ctx_proj(ctx, mod, w, k_col):
    b, c, d = ctx.shape
    ctx_row = mod.shape[0] - 1
    head_shape = jax.ShapeDtypeStruct((b, N_HEADS, c, HEAD_W), BF16)
    head_spec = pl.BlockSpec((1, N_HEADS, c, HEAD_W), lambda bi, j: (bi, 0, 0, 0))
    return pl.pallas_call(
        _ctx_kernel,
        grid=(b, 2),
        in_specs=[
            pl.BlockSpec((1, c, d), lambda bi, j: (bi, 0, 0)),
            pl.BlockSpec((1, 3, d), lambda bi, j: (ctx_row, 0, 0)),
            pl.BlockSpec((d, COL_TILE), lambda bi, j: (0, k_col + j)),
        ],
        out_specs=[head_spec, head_spec],
        out_shape=[head_shape, head_shape],
        scratch_shapes=[pltpu.VMEM((c, d), BF16)],
        compiler_params=pltpu.CompilerParams(
            dimension_semantics=("parallel", "arbitrary"), vmem_limit_bytes=VMEM_LIMIT),
        name="ctx_proj",
    )(ctx, mod, w)


def _lane_blocks_max(s):
    m = s[:, 0:LANES]
    for u in range(1, s.shape[1] // LANES):
        m = jnp.maximum(m, s[:, u * LANES:(u + 1) * LANES])
    return m


def _exp2_shifted(s, mb):
    blocks = [jnp.exp2(s[:, u * LANES:(u + 1) * LANES] - mb) for u in range(s.shape[1] // LANES)]
    return jnp.concatenate(blocks, axis=1).astype(BF16)


def _nt_dot(a, b):
    return lax.dot_general(a, b, (((1,), (1,)), ((), ())), preferred_element_type=F32)


def _attn_kernel(dl_ref, sub_ref, q_ref, kl_ref, kc_ref, vl_ref, vc_ref, o_ref,
                 vx_ref, sa_ref, sb_ref, ca_ref, cb_ref, ma_ref, mb_ref, *, tq, ck, lam_init):
    s_len = q_ref.shape[2]
    c_len = kc_ref.shape[2]
    n_chunks = s_len // ck
    n_tiles = s_len // tq
    bufs = ((sa_ref, ca_ref, ma_ref), (sb_ref, cb_ref, mb_ref))

    vx_ref[0:s_len, 0:HEAD_W] = vl_ref[0, 0]
    vx_ref[s_len:s_len + c_len, 0:HEAD_W] = vc_ref[0, 0]
    vx_ref[:, HEAD_W:2 * HEAD_W] = jnp.ones((s_len + c_len, HEAD_W), BF16)

    dl = dl_ref[...]
    lam = (jnp.exp(jnp.sum(dl[0:1] * dl[1:2], axis=1, keepdims=True))
           - jnp.exp(jnp.sum(dl[2:3] * dl[3:4], axis=1, keepdims=True)) + lam_init)
    sub = sub_ref[...] * (1.0 - lam_init)
    lane = lax.broadcasted_iota(jnp.int32, (1, HEAD_W), 1)
    comp0 = (lane % HEAD_DIM) < (HEAD_DIM // 2)

    def scores(t, buf):
        s_ref, sc_ref, m_ref = bufs[buf]
        q = q_ref[0, 0, pl.ds(pl.multiple_of(t * tq, tq), tq), :]
        zero = jnp.zeros_like(q)
        lhs = jnp.concatenate([jnp.where(comp0, q, zero), jnp.where(comp0, zero, q)], axis=0)
        sc = _nt_dot(lhs, kc_ref[0, 0])
        sc_ref[...] = sc
        m = _lane_blocks_max(sc)
        for c in range(n_chunks):
            s = _nt_dot(lhs, kl_ref[0, 0, c * ck:(c + 1) * ck, :])
            s_ref[c] = s
            m = jnp.maximum(m, _lane_blocks_max(s))
        m_ref[...] = jnp.broadcast_to(jnp.max(m, axis=1, keepdims=True), m_ref.shape)

    def weighted(t, buf):
        s_ref, sc_ref, m_ref = bufs[buf]
        acc = jnp.dot(_exp2_shifted(sc_ref[...], m_ref[...]), vx_ref[s_len:s_len + c_len, :],
                      preferred_element_type=F32)
        for c in range(n_chunks):
            p = _exp2_shifted(s_ref[c], m_ref[...])
            acc = acc + jnp.dot(p, vx_ref[c * ck:(c + 1) * ck, :], preferred_element_type=F32)
        o0 = acc[0:tq, 0:HEAD_W] / acc[0:tq, HEAD_W:2 * HEAD_W]
        o1 = acc[tq:2 * tq, 0:HEAD_W] / acc[tq:2 * tq, HEAD_W:2 * HEAD_W]
        o = o0 - lam * o1
        o = o * lax.rsqrt(jnp.mean(o * o, axis=-1, keepdims=True) + RMS_EPS) * sub
        o_ref[0, pl.ds(pl.multiple_of(t * tq, tq), tq), :] = o

    scores(0, 0)

    def pair(u, carry):
        t = 2 * u
        scores(t + 1, 1)
        weighted(t, 0)
        scores(t + 2, 0)
        weighted(t + 1, 1)
        return carry

    lax.fori_loop(0, n_tiles // 2 - 1, pair, 0)
    scores(n_tiles - 1, 1)
    weighted(n_tiles - 2, 0)
    weighted(n_tiles - 1, 1)


def _attention(dl, subln, q, kl, kc, vl, vc, lam_init):
    b, h, s, _ = q.shape
    c = kc.shape[2]
    tq = min(Q_TILE, s // 2)
    ck = min(KEY_CHUNK, s)
    assert s % (2 * tq) == 0 and s % ck == 0
    lat_spec = pl.BlockSpec((1, 1, s, HEAD_W), lambda bi, hi: (bi, hi, 0, 0))
    ctx_spec = pl.BlockSpec((1, 1, c, HEAD_W), lambda bi, hi: (bi, hi, 0, 0))
    s_buf = pltpu.VMEM((s // ck, 2 * tq, ck), F32)
    c_buf = pltpu.VMEM((2 * tq, c), F32)
    m_buf = pltpu.VMEM((2 * tq, LANES), F32)
    return pl.pallas_call(
        functools.partial(_attn_kernel, tq=tq, ck=ck, lam_init=lam_init),
        grid=(b, h),
        in_specs=[
            pl.BlockSpec(dl.shape, lambda bi, hi: (0, 0)),
            pl.BlockSpec(subln.shape, lambda bi, hi: (0, 0)),
            lat_spec, lat_spec, ctx_spec, lat_spec, ctx_spec,
        ],
        out_specs=pl.BlockSpec((1, s, HEAD_W), lambda bi, hi: (bi, 0, hi)),
        out_shape=jax.ShapeDtypeStruct((b, s, h * HEAD_W), F32),
        scratch_shapes=[pltpu.VMEM((s + c, 2 * HEAD_W), BF16),
                        s_buf, s_buf, c_buf, c_buf, m_buf, m_buf],
        compiler_params=pltpu.CompilerParams(
            dimension_semantics=("parallel", "parallel"), vmem_limit_bytes=VMEM_LIMIT),
        name="diff_attn",
    )(dl, subln, q, kl, kc, vl, vc)


def _out_kernel(xa_ref, gb_ref, gc_ref, za_ref, xap_ref, gcp_ref, xan_ref, gcn_ref,
                g_ref, zb_ref, o_ref, x_ref, mod_ref, cw_ref, woa_ref, wob_ref, wo_ref, fw_ref,
                out_ref, *, tm):
    i = pl.program_id(1)
    last = pl.num_programs(1) - 1
    d = x_ref.shape[2]

    u = gc_ref[0] * xa_ref[0]
    u_prev = jnp.where(i > 0, gcp_ref[0, 7:8, :] * xap_ref[0, 7:8, :], 0.0)
    u_next = jnp.where(i < last, gcn_ref[0, 0:1, :] * xan_ref[0, 0:1, :], 0.0)
    row = lax.broadcasted_iota(jnp.int32, (tm, 1), 0)
    u_dn = jnp.where(row == 0, u_prev, pltpu.roll(u, 1, 0))
    u_up = jnp.where(row == tm - 1, u_next, pltpu.roll(u, tm - 1, 0))
    y = cw_ref[0:1, :] * u_dn + cw_ref[1:2, :] * u + cw_ref[2:3, :] * u_up
    a_in = (gb_ref[0] * y * _silu(za_ref[0])).astype(BF16)
    y_a = jnp.dot(a_in, woa_ref[...], preferred_element_type=F32)

    b_in = (o_ref[0] * _silu(zb_ref[0])).astype(BF16)
    y_b = jnp.dot(b_in, wob_ref[...], preferred_element_type=F32)

    g = jax.nn.sigmoid(g_ref[0])
    merged = (g[:, 0:d] * y_a + g[:, d:2 * d] * y_b).astype(BF16)
    y_o = jnp.dot(merged, wo_ref[...], preferred_element_type=F32)

    xo = x_ref[0] + mod_ref[0, 2:3, :] * y_o
    out_ref[0] = xo * lax.rsqrt(jnp.mean(xo * xo, axis=-1, keepdims=True) + RMS_EPS) * fw_ref[...]


def _out_stage(pe, o, x, mod, conv_w, woa, wob, wo, fw):
    b, s, d = x.shape
    tm = min(OUT_TILE, s)
    t8 = tm // 8
    nblk8 = s // 8
    col = lambda cb: (lambda bi, i: (bi, i, cb))
    prev = lambda cb: (lambda bi, i: (bi, jnp.maximum(i * t8 - 1, 0), cb))
    nxt = lambda cb: (lambda bi, i: (bi, jnp.minimum((i + 1) * t8, nblk8 - 1), cb))
    const2 = lambda bi, i: (0, 0)
    resident = dict(pipeline_mode=pl.Buffered(1))
    return pl.pallas_call(
        functools.partial(_out_kernel, tm=tm),
        grid=(b, s // tm),
        in_specs=[
            pl.BlockSpec((1, tm, COL_TILE), col(0)),
            pl.BlockSpec((1, tm, COL_TILE), col(1)),
            pl.BlockSpec((1, tm, COL_TILE), col(2)),
            pl.BlockSpec((1, tm, COL_TILE), col(3)),
            pl.BlockSpec((1, 8, COL_TILE), prev(0)),
            pl.BlockSpec((1, 8, COL_TILE), prev(2)),
            pl.BlockSpec((1, 8, COL_TILE), nxt(0)),
            pl.BlockSpec((1, 8, COL_TILE), nxt(2)),
            pl.BlockSpec((1, tm, 2 * d), col(1)),
            pl.BlockSpec((1, tm, COL_TILE), col(8)),
            pl.BlockSpec((1, tm, COL_TILE), col(0)),
            pl.BlockSpec((1, tm, d), col(0)),
            pl.BlockSpec((1, 3, d), lambda bi, i: (bi, 0, 0)),
            pl.BlockSpec(conv_w.shape, const2),
            pl.BlockSpec(woa.shape, const2, **resident),
            pl.BlockSpec(wob.shape, const2, **resident),
            pl.BlockSpec(wo.shape, const2, **resident),
            pl.BlockSpec(fw.shape, const2),
        ],
        out_specs=pl.BlockSpec((1, tm, d), col(0)),
        out_shape=jax.ShapeDtypeStruct((b, s, d), x.dtype),
        compiler_params=pltpu.CompilerParams(
            dimension_semantics=("parallel", "parallel"), vmem_limit_bytes=VMEM_LIMIT),
        name="out_stage",
    )(pe, pe, pe, pe, pe, pe, pe, pe, pe, pe, o, x, mod, conv_w, woa, wob, wo, fw)


def _head_lane_order(w):
    lead = w.shape[:-1]
    w = w.reshape(lead + (N_HEADS, 2, 2, 2, AXIS_ROT // 2))
    n = len(lead)
    w = jnp.transpose(w, tuple(range(n)) + (n, n + 3, n + 1, n + 2, n + 4))
    return w.reshape(lead + (N_HEADS * HEAD_W,))


def _rope_tables(n_tokens):
    rows = n_tokens // GRID_W
    row = jnp.repeat(jnp.arange(rows, dtype=F32), GRID_W)
    col = jnp.tile(jnp.arange(GRID_W, dtype=F32), rows)
    inv_freq = ROPE_THETA ** (-jnp.arange(0, AXIS_ROT, 2, dtype=F32) / AXIS_ROT)
    ang = jnp.concatenate([row[:, None] * inv_freq, col[:, None] * inv_freq], axis=-1)
    ang = jnp.tile(ang, (1, 4))
    sign = jnp.where(jnp.arange(HEAD_W) < HEAD_W // 2, -1.0, 1.0).astype(F32)
    return jnp.cos(ang), jnp.sin(ang) * sign


def kernel(x, c, ctx, c_ctx, w_ada, b_ada, w_in, conv_w, diff_lambda, subln_w,
           w_out_a, w_out_b, w_o, final_norm_w):
    b, s, d = x.shape
    cw = conv_w.shape[-1]
    aw = w_out_b.shape[1]
    assert w_in.shape[0] == 1, "single-layer block"
    assert cw == COL_TILE and aw == COL_TILE == N_HEADS * HEAD_W and d == 2 * COL_TILE
    assert s % GRID_W == 0 and s % min(ROW_TILE, s) == 0 and ctx.shape[1] % LANES == 0
    lam_init = 0.8 - 0.6 * math.exp(-0.3 * 0)

    wi = w_in[0]
    a_end = 4 * cw
    wq, wk, wv, wzb = (wi[:, a_end + n * aw:a_end + (n + 1) * aw] for n in range(4))
    wg = wi[:, a_end + 4 * aw:]
    w = jnp.concatenate(
        [wi[:, :a_end], wg, wzb, _head_lane_order(wq), _head_lane_order(wk), wv],
        axis=1).astype(BF16)
    n_pe = (a_end + 2 * d + aw) // COL_TILE

    cc = jnp.concatenate([c, c_ctx[None, :]], axis=0)
    mod = _modulation(cc, w_ada[0], b_ada[0]).reshape(b + 1, 3, d)

    cos_t, sin_t = _rope_tables(s)
    pe, q, kl, vl = _inproj(x, mod, w, cos_t, sin_t, n_pe)
    kc, vc = _ctx_proj(ctx, mod, w, n_pe + 1)

    o = _attention(diff_lambda[0], subln_w[0].reshape(1, HEAD_W), q, kl, kc, vl, vc, lam_init)

    return _out_stage(pe, o, x, mod, conv_w[0],
                      w_out_a[0].astype(BF16), w_out_b[0].astype(BF16), w_o[0].astype(BF16),
                      final_norm_w.reshape(1, d))
```

```python
import functools
import math

import jax
import jax.numpy as jnp
from jax import lax
from jax.experimental import pallas as pl
from jax.experimental.pallas import tpu as pltpu

F32 = jnp.float32
BF16 = jnp.bfloat16

N_HEADS = 8
HEAD_DIM = 64
HEAD_W = 2 * HEAD_DIM
GRID_W = 64
ROPE_THETA = 10000.0
AXIS_ROT = HEAD_DIM // 2
RMS_EPS = 1e-6
QK_SCALE = HEAD_DIM ** -0.5
LOG2E = math.log2(math.e)
LANES = 128
COL_TILE = 1024

ROW_TILE = 1024
Q_TILE = 256
KEY_CHUNK = 512
OUT_TILE = 256
VMEM_LIMIT = 56 * 1024 * 1024


def _silu(x):
    return x * jax.nn.sigmoid(x)


def _rms_mod(x, shift, scale):
    ms = jnp.mean(x * x, axis=-1, keepdims=True)
    return (x * lax.rsqrt(ms + RMS_EPS)) * (1.0 + scale) + shift


def _mod_kernel(c_ref, w_ref, b_ref, o_ref):
    a = _silu(c_ref[...]).astype(BF16)
    o_ref[...] = jnp.dot(a, w_ref[...].astype(BF16), preferred_element_type=F32) + b_ref[...]


def _modulation(cc, w_ada, b_ada):
    n, d = cc.shape
    cols = w_ada.shape[1]
    return pl.pallas_call(
        _mod_kernel,
        grid=(cols // COL_TILE,),
        in_specs=[
            pl.BlockSpec((n, d), lambda j: (0, 0)),
            pl.BlockSpec((d, COL_TILE), lambda j: (0, j)),
            pl.BlockSpec((1, COL_TILE), lambda j: (0, j)),
        ],
        out_specs=pl.BlockSpec((n, COL_TILE), lambda j: (0, j)),
        out_shape=jax.ShapeDtypeStruct((n, cols), F32),
        compiler_params=pltpu.CompilerParams(
            dimension_semantics=("parallel",), vmem_limit_bytes=VMEM_LIMIT),
        name="adaln_mod",
    )(cc, w_ada, b_ada.reshape(1, cols))


def _store_heads(ref, t):
    for hh in range(N_HEADS):
        ref[0, 0, hh] = t[:, hh * HEAD_W:(hh + 1) * HEAD_W].astype(ref.dtype)


def _rope_heads(ref, t, cos, sin, scale):
    for hh in range(N_HEADS):
        th = t[:, hh * HEAD_W:(hh + 1) * HEAD_W]
        rot = pltpu.roll(th, HEAD_W // 2, 1)
        ref[0, 0, hh] = ((th * cos + rot * sin) * scale).astype(ref.dtype)


def _norm_to_scratch(x_ref, mod_ref, h_ref):
    shift = mod_ref[0, 0:1, :]
    scale = mod_ref[0, 1:2, :]
    h_ref[...] = _rms_mod(x_ref[0], shift, scale).astype(BF16)


def _inproj_kernel(x_ref, mod_ref, w_ref, cos_ref, sin_ref,
                   pe_ref, qkv_ref, h_ref, *, n_pe):
    j = pl.program_id(2)

    @pl.when(j == 0)
    def _():
        _norm_to_scratch(x_ref, mod_ref, h_ref)

    def proj():
        return jnp.dot(h_ref[...], w_ref[...], preferred_element_type=F32)

    @pl.when(j < n_pe)
    def _():
        pe_ref[0] = proj()

    @pl.when(j == n_pe)
    def _():
        _rope_heads(qkv_ref, proj(), cos_ref[...], sin_ref[...], QK_SCALE * LOG2E)

    @pl.when(j == n_pe + 1)
    def _():
        _rope_heads(qkv_ref, proj(), cos_ref[...], sin_ref[...], 1.0)

    @pl.when(j == n_pe + 2)
    def _():
        _store_heads(qkv_ref, proj())


def _inproj(x, mod, w, cos_t, sin_t, n_pe):
    b, s, d = x.shape
    tm = min(ROW_TILE, s)
    nj = n_pe + 3
    qkv_spec = pl.BlockSpec((1, 1, N_HEADS, tm, HEAD_W),
                            lambda bi, i, j: (bi, jnp.clip(j - n_pe, 0, 2), 0, i, 0))
    return pl.pallas_call(
        functools.partial(_inproj_kernel, n_pe=n_pe),
        grid=(b, s // tm, nj),
        in_specs=[
            pl.BlockSpec((1, tm, d), lambda bi, i, j: (bi, i, 0)),
            pl.BlockSpec((1, 3, d), lambda bi, i, j: (bi, 0, 0)),
            pl.BlockSpec((d, COL_TILE), lambda bi, i, j: (0, j)),
            pl.BlockSpec((tm, HEAD_W), lambda bi, i, j: (i, 0)),
            pl.BlockSpec((tm, HEAD_W), lambda bi, i, j: (i, 0)),
        ],
        out_specs=[
            pl.BlockSpec((1, tm, COL_TILE), lambda bi, i, j: (bi, i, jnp.minimum(j, n_pe - 1))),
            qkv_spec,
        ],
        out_shape=[
            jax.ShapeDtypeStruct((b, s, n_pe * COL_TILE), F32),
            jax.ShapeDtypeStruct((b, 3, N_HEADS, s, HEAD_W), BF16),
        ],
        scratch_shapes=[pltpu.VMEM((tm, d), BF16)],
        compiler_params=pltpu.CompilerParams(
            dimension_semantics=("parallel", "parallel", "arbitrary"),
            vmem_limit_bytes=VMEM_LIMIT),
        name="in_proj",
    )(x, mod, w, cos_t, sin_t)


def _ctx_kernel(x_ref, mod_ref, w_ref, kv_ref, h_ref):
    @pl.when(pl.program_id(1) == 0)
    def _():
        _norm_to_scratch(x_ref, mod_ref, h_ref)

    _store_heads(kv_ref, jnp.dot(h_ref[...], w_ref[...], preferred_element_type=F32))


def _ctx_proj(ctx, mod, w, k_col):
    b, c, d = ctx.shape
    ctx_row = mod.shape[0] - 1
    return pl.pallas_call(
        _ctx_kernel,
        grid=(b, 2),
        in_specs=[
            pl.BlockSpec((1, c, d), lambda bi, j: (bi, 0, 0)),
            pl.BlockSpec((1, 3, d), lambda bi, j: (ctx_row, 0, 0)),
            pl.BlockSpec((d, COL_TILE), lambda bi, j: (0, k_col + j)),
        ],
        out_specs=pl.BlockSpec((1, 1, N_HEADS, c, HEAD_W), lambda bi, j: (bi, j, 0, 0, 0)),
        out_shape=jax.ShapeDtypeStruct((b, 2, N_HEADS, c, HEAD_W), BF16),
        scratch_shapes=[pltpu.VMEM((c, d), BF16)],
        compiler_params=pltpu.CompilerParams(
            dimension_semantics=("parallel", "arbitrary"), vmem_limit_bytes=VMEM_LIMIT),
        name="ctx_proj",
    )(ctx, mod, w)


def _lane_blocks_max(s):
    m = s[:, 0:LANES]
    for u in range(1, s.shape[1] // LANES):
        m = jnp.maximum(m, s[:, u * LANES:(u + 1) * LANES])
    return m


def _exp2_shifted(s, mb):
    blocks = [jnp.exp2(s[:, u * LANES:(u + 1) * LANES] - mb) for u in range(s.shape[1] // LANES)]
    return jnp.concatenate(blocks, axis=1).astype(BF16)


def _nt_dot(a, b):
    return lax.dot_general(a, b, (((1,), (1,)), ((), ())), preferred_element_type=F32)


def _attn_kernel(dl_ref, sub_ref, q_ref, kl_ref, kc_ref, vl_ref, vc_ref, o_ref,
                 vx_ref, sa_ref, sb_ref, ca_ref, cb_ref, ma_ref, mb_ref, *, tq, ck, lam_init):
    s_len = q_ref.shape[0]
    c_len = kc_ref.shape[0]
    n_chunks = s_len // ck
    n_tiles = s_len // tq
    bufs = ((sa_ref, ca_ref, ma_ref), (sb_ref, cb_ref, mb_ref))

    vx_ref[0:s_len, 0:HEAD_W] = vl_ref[...]
    vx_ref[s_len:s_len + c_len, 0:HEAD_W] = vc_ref[...]
    vx_ref[:, HEAD_W:2 * HEAD_W] = jnp.ones((s_len + c_len, HEAD_W), BF16)

    dl = dl_ref[...]
    lam = (jnp.exp(jnp.sum(dl[0:1] * dl[1:2], axis=1, keepdims=True))
           - jnp.exp(jnp.sum(dl[2:3] * dl[3:4], axis=1, keepdims=True)) + lam_init)
    sub = sub_ref[...] * (1.0 - lam_init)
    lane = lax.broadcasted_iota(jnp.int32, (1, HEAD_W), 1)
    comp0 = (lane % HEAD_DIM) < (HEAD_DIM // 2)

    def scores(t, buf):
        s_ref, sc_ref, m_ref = bufs[buf]
        q = q_ref[pl.ds(pl.multiple_of(t * tq, tq), tq), :]
        zero = jnp.zeros_like(q)
        lhs = jnp.concatenate([jnp.where(comp0, q, zero), jnp.where(comp0, zero, q)], axis=0)
        sc = _nt_dot(lhs, kc_ref[...])
        sc_ref[...] = sc
        m = _lane_blocks_max(sc)
        for c in range(n_chunks):
            s = _nt_dot(lhs, kl_ref[c * ck:(c + 1) * ck, :])
            s_ref[c] = s
            m = jnp.maximum(m, _lane_blocks_max(s))
        m_ref[...] = jnp.broadcast_to(jnp.max(m, axis=1, keepdims=True), m_ref.shape)

    def weighted(t, buf):
        s_ref, sc_ref, m_ref = bufs[buf]
        acc = jnp.dot(_exp2_shifted(sc_ref[...], m_ref[...]), vx_ref[s_len:s_len + c_len, :],
                      preferred_element_type=F32)
        for c in range(n_chunks):
            p = _exp2_shifted(s_ref[c], m_ref[...])
            acc = acc + jnp.dot(p, vx_ref[c * ck:(c + 1) * ck, :], preferred_element_type=F32)
        o0 = acc[0:tq, 0:HEAD_W] / acc[0:tq, HEAD_W:2 * HEAD_W]
        o1 = acc[tq:2 * tq, 0:HEAD_W] / acc[tq:2 * tq, HEAD_W:2 * HEAD_W]
        o = o0 - lam * o1
        o = o * lax.rsqrt(jnp.mean(o * o, axis=-1, keepdims=True) + RMS_EPS) * sub
        o_ref[0, pl.ds(pl.multiple_of(t * tq, tq), tq), :] = o

    scores(0, 0)

    def pair(u, carry):
        t = 2 * u
        scores(t + 1, 1)
        weighted(t, 0)
        scores(t + 2, 0)
        weighted(t + 1, 1)
        return carry

    lax.fori_loop(0, n_tiles // 2 - 1, pair, 0)
    scores(n_tiles - 1, 1)
    weighted(n_tiles - 2, 0)
    weighted(n_tiles - 1, 1)


def _attention(dl, subln, qkv, kv_ctx, lam_init):
    b, _, h, s, _ = qkv.shape
    c = kv_ctx.shape[3]
    tq = min(Q_TILE, s // 2)
    ck = min(KEY_CHUNK, s)
    assert s % (2 * tq) == 0 and s % ck == 0
    head = lambda n, which: pl.BlockSpec((None, None, None, n, HEAD_W),
                                         lambda bi, hi: (bi, which, hi, 0, 0))
    s_buf = pltpu.VMEM((s // ck, 2 * tq, ck), F32)
    c_buf = pltpu.VMEM((2 * tq, c), F32)
    m_buf = pltpu.VMEM((2 * tq, LANES), F32)
    return pl.pallas_call(
        functools.partial(_attn_kernel, tq=tq, ck=ck, lam_init=lam_init),
        grid=(b, h),
        in_specs=[
            pl.BlockSpec(dl.shape, lambda bi, hi: (0, 0)),
            pl.BlockSpec(subln.shape, lambda bi, hi: (0, 0)),
            head(s, 0), head(s, 1), head(c, 0), head(s, 2), head(c, 1),
        ],
        out_specs=pl.BlockSpec((1, s, HEAD_W), lambda bi, hi: (bi, 0, hi)),
        out_shape=jax.ShapeDtypeStruct((b, s, h * HEAD_W), F32),
        scratch_shapes=[pltpu.VMEM((s + c, 2 * HEAD_W), BF16),
                        s_buf, s_buf, c_buf, c_buf, m_buf, m_buf],
        compiler_params=pltpu.CompilerParams(
            dimension_semantics=("parallel", "parallel"), vmem_limit_bytes=VMEM_LIMIT),
        name="diff_attn",
    )(dl, subln, qkv, qkv, kv_ctx, qkv, kv_ctx)


def _out_kernel(xa_ref, gb_ref, gc_ref, za_ref, xap_ref, gcp_ref, xan_ref, gcn_ref,
                g_ref, zb_ref, o_ref, x_ref, mod_ref, cw_ref, woa_ref, wob_ref, wo_ref, fw_ref,
                out_ref, *, tm):
    i = pl.program_id(1)
    last = pl.num_programs(1) - 1
    d = x_ref.shape[2]

    u = gc_ref[0] * xa_ref[0]
    u_prev = jnp.where(i > 0, gcp_ref[0, 7:8, :] * xap_ref[0, 7:8, :], 0.0)
    u_next = jnp.where(i < last, gcn_ref[0, 0:1, :] * xan_ref[0, 0:1, :], 0.0)
    row = lax.broadcasted_iota(jnp.int32, (tm, 1), 0)
    u_dn = jnp.where(row == 0, u_prev, pltpu.roll(u, 1, 0))
    u_up = jnp.where(row == tm - 1, u_next, pltpu.roll(u, tm - 1, 0))
    y = cw_ref[0:1, :] * u_dn + cw_ref[1:2, :] * u + cw_ref[2:3, :] * u_up
    a_in = (gb_ref[0] * y * _silu(za_ref[0])).astype(BF16)
    y_a = jnp.dot(a_in, woa_ref[...], preferred_element_type=F32)

    b_in = (o_ref[0] * _silu(zb_ref[0])).astype(BF16)
    y_b = jnp.dot(b_in, wob_ref[...], preferred_element_type=F32)

    g = jax.nn.sigmoid(g_ref[0])
    merged = (g[:, 0:d] * y_a + g[:, d:2 * d] * y_b).astype(BF16)
    y_o = jnp.dot(merged, wo_ref[...], preferred_element_type=F32)

    xo = x_ref[0] + mod_ref[0, 2:3, :] * y_o
    out_ref[0] = xo * lax.rsqrt(jnp.mean(xo * xo, axis=-1, keepdims=True) + RMS_EPS) * fw_ref[...]


def _out_stage(pe, o, x, mod, conv_w, woa, wob, wo, fw):
    b, s, d = x.shape
    tm = min(OUT_TILE, s)
    t8 = tm // 8
    nblk8 = s // 8
    col = lambda cb: (lambda bi, i: (bi, i, cb))
    prev = lambda cb: (lambda bi, i: (bi, jnp.maximum(i * t8 - 1, 0), cb))
    nxt = lambda cb: (lambda bi, i: (bi, jnp.minimum((i + 1) * t8, nblk8 - 1), cb))
    const2 = lambda bi, i: (0, 0)
    resident = dict(pipeline_mode=pl.Buffered(1))
    return pl.pallas_call(
        functools.partial(_out_kernel, tm=tm),
        grid=(b, s // tm),
        in_specs=[
            pl.BlockSpec((1, tm, COL_TILE), col(0)),
            pl.BlockSpec((1, tm, COL_TILE), col(1)),
            pl.BlockSpec((1, tm, COL_TILE), col(2)),
            pl.BlockSpec((1, tm, COL_TILE), col(3)),
            pl.BlockSpec((1, 8, COL_TILE), prev(0)),
            pl.BlockSpec((1, 8, COL_TILE), prev(2)),
            pl.BlockSpec((1, 8, COL_TILE), nxt(0)),
            pl.BlockSpec((1, 8, COL_TILE), nxt(2)),
            pl.BlockSpec((1, tm, 2 * d), col(1)),
            pl.BlockSpec((1, tm, COL_TILE), col(8)),
            pl.BlockSpec((1, tm, COL_TILE), col(0)),
            pl.BlockSpec((1, tm, d), col(0)),
            pl.BlockSpec((1, 3, d), lambda bi, i: (bi, 0, 0)),
            pl.BlockSpec(conv_w.shape, const2),
            pl.BlockSpec(woa.shape, const2, **resident),
            pl.BlockSpec(wob.shape, const2, **resident),
            pl.BlockSpec(wo.shape, const2, **resident),
            pl.BlockSpec(fw.shape, const2),
        ],
        out_specs=pl.BlockSpec((1, tm, d), col(0)),
        out_shape=jax.ShapeDtypeStruct((b, s, d), x.dtype),
        compiler_params=pltpu.CompilerParams(
            dimension_semantics=("parallel", "parallel"), vmem_limit_bytes=VMEM_LIMIT),
        name="out_stage",
    )(pe, pe, pe, pe, pe, pe, pe, pe, pe, pe, o, x, mod, conv_w, woa, wob, wo, fw)


def _head_lane_order(w):
    lead = w.shape[:-1]
    w = w.reshape(lead + (N_HEADS, 2, 2, 2, AXIS_ROT // 2))
    n = len(lead)
    w = jnp.transpose(w, tuple(range(n)) + (n, n + 3, n + 1, n + 2, n + 4))
    return w.reshape(lead + (N_HEADS * HEAD_W,))


def _rope_tables(n_tokens):
    rows = n_tokens // GRID_W
    row = jnp.repeat(jnp.arange(rows, dtype=F32), GRID_W)
    col = jnp.tile(jnp.arange(GRID_W, dtype=F32), rows)
    inv_freq = ROPE_THETA ** (-jnp.arange(0, AXIS_ROT, 2, dtype=F32) / AXIS_ROT)
    ang = jnp.concatenate([row[:, None] * inv_freq, col[:, None] * inv_freq], axis=-1)
    ang = jnp.tile(ang, (1, 4))
    sign = jnp.where(jnp.arange(HEAD_W) < HEAD_W // 2, -1.0, 1.0).astype(F32)
    return jnp.cos(ang), jnp.sin(ang) * sign


def kernel(x, c, ctx, c_ctx, w_ada, b_ada, w_in, conv_w, diff_lambda, subln_w,
           w_out_a, w_out_b, w_o, final_norm_w):
    b, s, d = x.shape
    cw = conv_w.shape[-1]
    aw = w_out_b.shape[1]
    assert w_in.shape[0] == 1, "single-layer block"
    assert cw == COL_TILE and aw == COL_TILE == N_HEADS * HEAD_W and d == 2 * COL_TILE
    assert s % GRID_W == 0 and s % min(ROW_TILE, s) == 0 and ctx.shape[1] % LANES == 0
    lam_init = 0.8 - 0.6 * math.exp(-0.3 * 0)

    wi = w_in[0]
    a_end = 4 * cw
    wq, wk, wv, wzb = (wi[:, a_end + n * aw:a_end + (n + 1) * aw] for n in range(4))
    wg = wi[:, a_end + 4 * aw:]
    w = jnp.concatenate(
        [wi[:, :a_end], wg, wzb, _head_lane_order(wq), _head_lane_order(wk), wv],
        axis=1).astype(BF16)
    n_pe = (a_end + 2 * d + aw) // COL_TILE

    cc = jnp.concatenate([c, c_ctx[None, :]], axis=0)
    mod = _modulation(cc, w_ada[0], b_ada[0]).reshape(b + 1, 3, d)

    cos_t, sin_t = _rope_tables(s)
    pe, qkv = _inproj(x, mod, w, cos_t, sin_t, n_pe)
    kv_ctx = _ctx_proj(ctx, mod, w, n_pe + 1)

    o = _attention(diff_lambda[0], subln_w[0].reshape(1, HEAD_W), qkv, kv_ctx, lam_init)

    return _out_stage(pe, o, x, mod, conv_w[0],
                      w_out_a[0].astype(BF16), w_out_b[0].astype(BF16), w_o[0].astype(BF16),
                      final_norm_w.reshape(1, d))
```

```python
import functools
import math

import jax
import jax.numpy as jnp
from jax import lax
from jax.experimental import pallas as pl
from jax.experimental.pallas import tpu as pltpu

F32 = jnp.float32
BF16 = jnp.bfloat16

N_HEADS = 8
HEAD_DIM = 64
HEAD_W = 2 * HEAD_DIM
GRID_W = 64
ROPE_THETA = 10000.0
AXIS_ROT = HEAD_DIM // 2
RMS_EPS = 1e-6
QK_SCALE = HEAD_DIM ** -0.5
LOG2E = math.log2(math.e)
LANES = 128
COL_TILE = 1024

ROW_TILE = 1024
Q_TILE = 256
KEY_CHUNK = 512
KEY_GROUP = 4
OUT_TILE = 512
VMEM_LIMIT = 56 * 1024 * 1024


def _silu(x):
    return x * jax.nn.sigmoid(x)


def _rms_mod(x, shift, scale):
    ms = jnp.mean(x * x, axis=-1, keepdims=True)
    return (x * lax.rsqrt(ms + RMS_EPS)) * (1.0 + scale) + shift


def _mod_kernel(c_ref, w_ref, b_ref, o_ref):
    a = _silu(c_ref[...]).astype(BF16)
    o_ref[...] = jnp.dot(a, w_ref[...].astype(BF16), preferred_element_type=F32) + b_ref[...]


def _modulation(cc, w_ada, b_ada):
    n, d = cc.shape
    cols = w_ada.shape[1]
    return pl.pallas_call(
        _mod_kernel,
        grid=(cols // COL_TILE,),
        in_specs=[
            pl.BlockSpec((n, d), lambda j: (0, 0)),
            pl.BlockSpec((d, COL_TILE), lambda j: (0, j)),
            pl.BlockSpec((1, COL_TILE), lambda j: (0, j)),
        ],
        out_specs=pl.BlockSpec((n, COL_TILE), lambda j: (0, j)),
        out_shape=jax.ShapeDtypeStruct((n, cols), F32),
        compiler_params=pltpu.CompilerParams(
            dimension_semantics=("parallel",), vmem_limit_bytes=VMEM_LIMIT),
        name="adaln_mod",
    )(cc, w_ada, b_ada.reshape(1, cols))


def _store_heads(ref, t):
    for hh in range(N_HEADS):
        ref[0, 0, hh] = t[:, hh * HEAD_W:(hh + 1) * HEAD_W].astype(ref.dtype)


def _rope_heads(ref, t, cos, sin, scale):
    for hh in range(N_HEADS):
        th = t[:, hh * HEAD_W:(hh + 1) * HEAD_W]
        rot = pltpu.roll(th, HEAD_W // 2, 1)
        ref[0, 0, hh] = ((th * cos + rot * sin) * scale).astype(ref.dtype)


def _norm_to_scratch(x_ref, mod_ref, h_ref):
    shift = mod_ref[0, 0:1, :]
    scale = mod_ref[0, 1:2, :]
    h_ref[...] = _rms_mod(x_ref[0], shift, scale).astype(BF16)


def _inproj_kernel(x_ref, mod_ref, w_ref, cos_ref, sin_ref,
                   ug_ref, gz_ref, qkv_ref, h_ref, *, n_pair, n_act):
    j = pl.program_id(2)
    half = COL_TILE // 2

    @pl.when(j == 0)
    def _():
        _norm_to_scratch(x_ref, mod_ref, h_ref)

    def proj():
        return jnp.dot(h_ref[...], w_ref[...], preferred_element_type=F32)

    @pl.when(j < n_pair // 2)
    def _():
        t = proj()
        ug_ref[0] = (t[:, half:] * t[:, :half]).astype(ug_ref.dtype)

    @pl.when((j >= n_pair // 2) & (j < n_pair))
    def _():
        t = proj()
        ug_ref[0] = (t[:, :half] * _silu(t[:, half:])).astype(ug_ref.dtype)

    @pl.when((j >= n_pair) & (j < n_pair + n_act - 1))
    def _():
        gz_ref[0] = jax.nn.sigmoid(proj()).astype(gz_ref.dtype)

    @pl.when(j == n_pair + n_act - 1)
    def _():
        gz_ref[0] = _silu(proj()).astype(gz_ref.dtype)

    @pl.when(j == n_pair + n_act)
    def _():
        _rope_heads(qkv_ref, proj(), cos_ref[...], sin_ref[...], QK_SCALE * LOG2E)

    @pl.when(j == n_pair + n_act + 1)
    def _():
        _rope_heads(qkv_ref, proj(), cos_ref[...], sin_ref[...], 1.0)

    @pl.when(j == n_pair + n_act + 2)
    def _():
        _store_heads(qkv_ref, proj())


def _inproj(x, mod, w, cos_t, sin_t, n_pair, n_act):
    b, s, d = x.shape
    tm = min(ROW_TILE, s)
    n_pe = n_pair + n_act
    nj = n_pe + 3
    half = COL_TILE // 2
    qkv_spec = pl.BlockSpec((1, 1, N_HEADS, tm, HEAD_W),
                            lambda bi, i, j: (bi, jnp.clip(j - n_pe, 0, 2), 0, i, 0))
    return pl.pallas_call(
        functools.partial(_inproj_kernel, n_pair=n_pair, n_act=n_act),
        grid=(b, s // tm, nj),
        in_specs=[
            pl.BlockSpec((1, tm, d), lambda bi, i, j: (bi, i, 0)),
            pl.BlockSpec((1, 3, d), lambda bi, i, j: (bi, 0, 0)),
            pl.BlockSpec((d, COL_TILE), lambda bi, i, j: (0, j)),
            pl.BlockSpec((tm, HEAD_W), lambda bi, i, j: (i, 0)),
            pl.BlockSpec((tm, HEAD_W), lambda bi, i, j: (i, 0)),
        ],
        out_specs=[
            pl.BlockSpec((1, tm, half), lambda bi, i, j: (bi, i, jnp.minimum(j, n_pair - 1))),
            pl.BlockSpec((1, tm, COL_TILE),
                         lambda bi, i, j: (bi, i, jnp.clip(j - n_pair, 0, n_act - 1))),
            qkv_spec,
        ],
        out_shape=[
            jax.ShapeDtypeStruct((b, s, n_pair * half), BF16),
            jax.ShapeDtypeStruct((b, s, n_act * COL_TILE), BF16),
            jax.ShapeDtypeStruct((b, 3, N_HEADS, s, HEAD_W), BF16),
        ],
        scratch_shapes=[pltpu.VMEM((tm, d), BF16)],
        compiler_params=pltpu.CompilerParams(
            dimension_semantics=("parallel", "parallel", "arbitrary"),
            vmem_limit_bytes=VMEM_LIMIT),
        name="in_proj",
    )(x, mod, w, cos_t, sin_t)


def _ctx_kernel(x_ref, mod_ref, w_ref, kv_ref, h_ref):
    @pl.when(pl.program_id(1) == 0)
    def _():
        _norm_to_scratch(x_ref, mod_ref, h_ref)

    _store_heads(kv_ref, jnp.dot(h_ref[...], w_ref[...], preferred_element_type=F32))


def _ctx_proj(ctx, mod, w, k_col):
    b, c, d = ctx.shape
    ctx_row = mod.shape[0] - 1
    return pl.pallas_call(
        _ctx_kernel,
        grid=(b, 2),
        in_specs=[
            pl.BlockSpec((1, c, d), lambda bi, j: (bi, 0, 0)),
            pl.BlockSpec((1, 3, d), lambda bi, j: (ctx_row, 0, 0)),
            pl.BlockSpec((d, COL_TILE), lambda bi, j: (0, k_col + j)),
        ],
        out_specs=pl.BlockSpec((1, 1, N_HEADS, c, HEAD_W), lambda bi, j: (bi, j, 0, 0, 0)),
        out_shape=jax.ShapeDtypeStruct((b, 2, N_HEADS, c, HEAD_W), BF16),
        scratch_shapes=[pltpu.VMEM((c, d), BF16)],
        compiler_params=pltpu.CompilerParams(
            dimension_semantics=("parallel", "arbitrary"), vmem_limit_bytes=VMEM_LIMIT),
        name="ctx_proj",
    )(ctx, mod, w)


def _lane_blocks_max(s):
    m = s[:, 0:LANES]
    for u in range(1, s.shape[1] // LANES):
        m = jnp.maximum(m, s[:, u * LANES:(u + 1) * LANES])
    return m


def _exp2_shifted(s, mb):
    blocks = [jnp.exp2(s[:, u * LANES:(u + 1) * LANES] - mb) for u in range(s.shape[1] // LANES)]
    part = blocks[0]
    for blk in blocks[1:]:
        part = part + blk
    return jnp.concatenate(blocks, axis=1).astype(BF16), part


def _diff_terms(p, f, tq):
    blocks = [p[0:tq, u * LANES:(u + 1) * LANES] - f * p[tq:2 * tq, u * LANES:(u + 1) * LANES]
              for u in range(p.shape[1] // LANES)]
    return jnp.concatenate(blocks, axis=1)


def _nt_dot(a, b):
    return lax.dot_general(a, b, (((1,), (1,)), ((), ())), preferred_element_type=F32)


def _attn_kernel(dl_ref, sub_ref, q_ref, kl_ref, kc_ref, vl_ref, vc_ref, o_ref,
                 s_ref, sc_ref, p_ref, pc_ref, l_ref, ma_ref, mb_ref, fa_ref, fb_ref,
                 ra_ref, rb_ref, lhs_ref, acc_ref, *, tq, ck, kg, lam_init):
    s_len = q_ref.shape[0]
    n_tiles = s_len // tq
    m_bufs = (ma_ref, mb_ref)
    f_bufs = ((fa_ref, ra_ref), (fb_ref, rb_ref))

    dl = dl_ref[...]
    lam = (jnp.exp(jnp.sum(dl[0:1] * dl[1:2], axis=1, keepdims=True))
           - jnp.exp(jnp.sum(dl[2:3] * dl[3:4], axis=1, keepdims=True)) + lam_init)
    sub = sub_ref[...] * (1.0 - lam_init)
    lane = lax.broadcasted_iota(jnp.int32, (1, HEAD_W), 1)
    comp0 = (lane % HEAD_DIM) < (HEAD_DIM // 2)

    n_groups = s_len // (ck * kg)

    def step(t_out, t_exp, t_qk, par):
        f_ref, r_ref = f_bufs[par]
        m_exp, m_qk = m_bufs[1 - par], m_bufs[par]
        if t_qk is not None:
            q = q_ref[pl.ds(pl.multiple_of(t_qk * tq, tq), tq), :]
            zero = jnp.zeros_like(q)
            lhs_ref[0:tq, :] = jnp.where(comp0, q, zero)
            lhs_ref[tq:2 * tq, :] = jnp.where(comp0, zero, q)

        def chunk(k_c, v_c, s_c, p_c, first):
            if t_out is not None:
                part = jnp.dot(_diff_terms(p_c[...], f_ref[...], tq), v_c[...],
                               preferred_element_type=F32)
                acc_ref[...] = part if first else acc_ref[...] + part
            if t_exp is not None:
                p, part = _exp2_shifted(s_c[...], m_exp[...])
                p_c[...] = p
                l_ref[...] = part if first else l_ref[...] + part
            if t_qk is not None:
                s = _nt_dot(lhs_ref[...], k_c[...])
                s_c[...] = s
                smax = _lane_blocks_max(s)
                m_qk[...] = smax if first else jnp.maximum(m_qk[...], smax)

        chunk(kc_ref, vc_ref, sc_ref, pc_ref, True)

        def group(g, carry):
            for j in range(kg):
                c = g * kg + j
                rows = pl.ds(pl.multiple_of(c * ck, ck), ck)
                chunk(kl_ref.at[rows], vl_ref.at[rows], s_ref.at[c], p_ref.at[c], False)
            return carry

        lax.fori_loop(0, n_groups, group, 0)
        if t_out is not None:
            o = acc_ref[...] * r_ref[...]
            o = o * lax.rsqrt(jnp.mean(o * o, axis=-1, keepdims=True) + RMS_EPS) * sub
            o_ref[0, pl.ds(pl.multiple_of(t_out * tq, tq), tq), :] = o
        if t_exp is not None:
            l = jnp.sum(l_ref[...], axis=1, keepdims=True)
            l0, l1 = l[0:tq], l[tq:2 * tq]
            f_next, r_next = f_bufs[1 - par]
            f_next[...] = jnp.broadcast_to(lam * l0 / l1, f_next.shape).astype(BF16)
            r_next[...] = jnp.broadcast_to(1.0 / l0, r_next.shape)
        if t_qk is not None:
            m_qk[...] = jnp.broadcast_to(jnp.max(m_qk[...], axis=1, keepdims=True), m_qk.shape)

    step(None, None, 0, 0)
    step(None, 0, 1, 1)

    def pair(u, carry):
        t = 2 * u
        step(t, t + 1, t + 2, 0)
        step(t + 1, t + 2, t + 3, 1)
        return carry

    lax.fori_loop(0, n_tiles // 2 - 1, pair, 0)
    step(n_tiles - 2, n_tiles - 1, None, 0)
    step(n_tiles - 1, None, None, 1)


def _attention(dl, subln, qkv, kv_ctx, lam_init):
    b, _, h, s, _ = qkv.shape
    c = kv_ctx.shape[3]
    tq = min(Q_TILE, s // 2)
    ck = min(KEY_CHUNK, s)
    kg = min(KEY_GROUP, s // ck)
    assert s % (2 * tq) == 0 and s % (ck * kg) == 0
    head = lambda n, which: pl.BlockSpec((None, None, None, n, HEAD_W),
                                         lambda bi, hi: (bi, which, hi, 0, 0))
    m_buf = pltpu.VMEM((2 * tq, LANES), F32)
    f_buf = pltpu.VMEM((tq, LANES), BF16)
    r_buf = pltpu.VMEM((tq, LANES), F32)
    return pl.pallas_call(
        functools.partial(_attn_kernel, tq=tq, ck=ck, kg=kg, lam_init=lam_init),
        grid=(b, h),
        in_specs=[
            pl.BlockSpec(dl.shape, lambda bi, hi: (0, 0)),
            pl.BlockSpec(subln.shape, lambda bi, hi: (0, 0)),
            head(s, 0), head(s, 1), head(c, 0), head(s, 2), head(c, 1),
        ],
        out_specs=pl.BlockSpec((1, s, HEAD_W), lambda bi, hi: (bi, 0, hi)),
        out_shape=jax.ShapeDtypeStruct((b, s, h * HEAD_W), F32),
        scratch_shapes=[
            pltpu.VMEM((s // ck, 2 * tq, ck), F32), pltpu.VMEM((2 * tq, c), F32),
            pltpu.VMEM((s // ck, 2 * tq, ck), BF16), pltpu.VMEM((2 * tq, c), BF16),
            m_buf, m_buf, m_buf, f_buf, f_buf, r_buf, r_buf,
            pltpu.VMEM((2 * tq, HEAD_W), BF16), r_buf,
        ],
        compiler_params=pltpu.CompilerParams(
            dimension_semantics=("parallel", "parallel"), vmem_limit_bytes=VMEM_LIMIT),
        name="diff_attn",
    )(dl, subln, qkv, qkv, kv_ctx, qkv, kv_ctx)


BF16_ROWS = 16


def _out_kernel(u_ref, g2_ref, up_ref, un_ref, sig_ref, szb_ref, o_ref, x_ref, mod_ref, cw_ref,
                woa_ref, wob_ref, wo_ref, fw_ref, out_ref, *, tm):
    i = pl.program_id(1)
    last = pl.num_programs(1) - 1
    d = x_ref.shape[2]

    u = u_ref[0].astype(F32)
    u_prev = jnp.where(i > 0, up_ref[0, BF16_ROWS - 1:BF16_ROWS, :].astype(F32), 0.0)
    u_next = jnp.where(i < last, un_ref[0, 0:1, :].astype(F32), 0.0)
    row = lax.broadcasted_iota(jnp.int32, (tm, 1), 0)
    u_dn = jnp.where(row == 0, u_prev, pltpu.roll(u, 1, 0))
    u_up = jnp.where(row == tm - 1, u_next, pltpu.roll(u, tm - 1, 0))
    y = cw_ref[0:1, :] * u_dn + cw_ref[1:2, :] * u + cw_ref[2:3, :] * u_up
    a_in = (g2_ref[0].astype(F32) * y).astype(BF16)
    y_a = jnp.dot(a_in, woa_ref[...], preferred_element_type=F32)

    b_in = (o_ref[0] * szb_ref[0].astype(F32)).astype(BF16)
    y_b = jnp.dot(b_in, wob_ref[...], preferred_element_type=F32)

    merged = (sig_ref[0, :, 0:d].astype(F32) * y_a
              + sig_ref[0, :, d:2 * d].astype(F32) * y_b).astype(BF16)
    y_o = jnp.dot(merged, wo_ref[...], preferred_element_type=F32)

    xo = x_ref[0] + mod_ref[0, 2:3, :] * y_o
    out_ref[0] = xo * lax.rsqrt(jnp.mean(xo * xo, axis=-1, keepdims=True) + RMS_EPS) * fw_ref[...]


def _out_stage(ug, gz, o, x, mod, conv_w, woa, wob, wo, fw):
    b, s, d = x.shape
    cw = conv_w.shape[-1]
    tm = min(OUT_TILE, s)
    tb = tm // BF16_ROWS
    nblk = s // BF16_ROWS
    col = lambda cb: (lambda bi, i: (bi, i, cb))
    const2 = lambda bi, i: (0, 0)
    resident = dict(pipeline_mode=pl.Buffered(1))
    return pl.pallas_call(
        functools.partial(_out_kernel, tm=tm),
        grid=(b, s // tm),
        in_specs=[
            pl.BlockSpec((1, tm, cw), col(0)),
            pl.BlockSpec((1, tm, cw), col(1)),
            pl.BlockSpec((1, BF16_ROWS, cw), lambda bi, i: (bi, jnp.maximum(i * tb - 1, 0), 0)),
            pl.BlockSpec((1, BF16_ROWS, cw),
                         lambda bi, i: (bi, jnp.minimum((i + 1) * tb, nblk - 1), 0)),
            pl.BlockSpec((1, tm, 2 * d), col(0)),
            pl.BlockSpec((1, tm, COL_TILE), col(2 * d // COL_TILE)),
            pl.BlockSpec((1, tm, COL_TILE), col(0)),
            pl.BlockSpec((1, tm, d), col(0)),
            pl.BlockSpec((1, 3, d), lambda bi, i: (bi, 0, 0)),
            pl.BlockSpec(conv_w.shape, const2),
            pl.BlockSpec(woa.shape, const2, **resident),
            pl.BlockSpec(wob.shape, const2, **resident),
            pl.BlockSpec(wo.shape, const2, **resident),
            pl.BlockSpec(fw.shape, const2),
        ],
        out_specs=pl.BlockSpec((1, tm, d), col(0)),
        out_shape=jax.ShapeDtypeStruct((b, s, d), x.dtype),
        compiler_params=pltpu.CompilerParams(
            dimension_semantics=("parallel", "parallel"), vmem_limit_bytes=VMEM_LIMIT),
        name="out_stage",
    )(ug, ug, ug, ug, gz, gz, o, x, mod, conv_w, woa, wob, wo, fw)


def _head_lane_order(w):
    lead = w.shape[:-1]
    w = w.reshape(lead + (N_HEADS, 2, 2, 2, AXIS_ROT // 2))
    n = len(lead)
    w = jnp.transpose(w, tuple(range(n)) + (n, n + 3, n + 1, n + 2, n + 4))
    return w.reshape(lead + (N_HEADS * HEAD_W,))


def _rope_tables(n_tokens):
    rows = n_tokens // GRID_W
    row = jnp.repeat(jnp.arange(rows, dtype=F32), GRID_W)
    col = jnp.tile(jnp.arange(GRID_W, dtype=F32), rows)
    inv_freq = ROPE_THETA ** (-jnp.arange(0, AXIS_ROT, 2, dtype=F32) / AXIS_ROT)
    ang = jnp.concatenate([row[:, None] * inv_freq, col[:, None] * inv_freq], axis=-1)
    ang = jnp.tile(ang, (1, 4))
    sign = jnp.where(jnp.arange(HEAD_W) < HEAD_W // 2, -1.0, 1.0).astype(F32)
    return jnp.cos(ang), jnp.sin(ang) * sign


def kernel(x, c, ctx, c_ctx, w_ada, b_ada, w_in, conv_w, diff_lambda, subln_w,
           w_out_a, w_out_b, w_o, final_norm_w):
    b, s, d = x.shape
    cw = conv_w.shape[-1]
    aw = w_out_b.shape[1]
    assert w_in.shape[0] == 1, "single-layer block"
    assert cw == COL_TILE and aw == COL_TILE == N_HEADS * HEAD_W and d == 2 * COL_TILE
    assert s % GRID_W == 0 and s % min(ROW_TILE, s) == 0 and ctx.shape[1] % LANES == 0
    lam_init = 0.8 - 0.6 * math.exp(-0.3 * 0)

    wi = w_in[0]
    half = COL_TILE // 2
    a_end = 4 * cw
    wxa, wgb, wgc, wza = (wi[:, n * cw:(n + 1) * cw] for n in range(4))
    wq, wk, wv, wzb = (wi[:, a_end + n * aw:a_end + (n + 1) * aw] for n in range(4))
    wg = wi[:, a_end + 4 * aw:]
    pairs = []
    for first, second in ((wxa, wgc), (wgb, wza)):
        for n in range(cw // half):
            pairs += [first[:, n * half:(n + 1) * half], second[:, n * half:(n + 1) * half]]
    w = jnp.concatenate(
        pairs + [wg, wzb, _head_lane_order(wq), _head_lane_order(wk), wv], axis=1).astype(BF16)
    n_pair = 4 * cw // COL_TILE
    n_act = (2 * d + aw) // COL_TILE

    cc = jnp.concatenate([c, c_ctx[None, :]], axis=0)
    mod = _modulation(cc, w_ada[0], b_ada[0]).reshape(b + 1, 3, d)

    cos_t, sin_t = _rope_tables(s)
    ug, gz, qkv = _inproj(x, mod, w, cos_t, sin_t, n_pair, n_act)
    kv_ctx = _ctx_proj(ctx, mod, w, n_pair + n_act + 1)

    o = _attention(diff_lambda[0], subln_w[0].reshape(1, HEAD_W), qkv, kv_ctx, lam_init)

    return _out_stage(ug, gz, o, x, mod, conv_w[0],
                      w_out_a[0].astype(BF16), w_out_b[0].astype(BF16), w_o[0].astype(BF16),
                      final_norm_w.reshape(1, d))
```

```python
import functools
import math

import jax
import jax.numpy as jnp
from jax import lax
from jax.experimental import pallas as pl
from jax.experimental.pallas import tpu as pltpu

F32 = jnp.float32
BF16 = jnp.bfloat16

N_HEADS = 8
HEAD_DIM = 64
HEAD_W = 2 * HEAD_DIM
GRID_W = 64
ROPE_THETA = 10000.0
AXIS_ROT = HEAD_DIM // 2
RMS_EPS = 1e-6
QK_SCALE = HEAD_DIM ** -0.5
LOG2E = math.log2(math.e)
LANES = 128
COL_TILE = 1024

ROW_TILE = 1024
Q_TILE = 256
KEY_CHUNK = 512
KEY_GROUP = 4
OUT_TILE = 512
VMEM_LIMIT = 56 * 1024 * 1024


def _silu(x):
    return x * jax.nn.sigmoid(x)


def _rms_mod(x, shift, scale):
    ms = jnp.mean(x * x, axis=-1, keepdims=True)
    return (x * lax.rsqrt(ms + RMS_EPS)) * (1.0 + scale) + shift


def _mod_kernel(c_ref, w_ref, b_ref, o_ref):
    a = _silu(c_ref[...]).astype(BF16)
    o_ref[...] = jnp.dot(a, w_ref[...].astype(BF16), preferred_element_type=F32) + b_ref[...]


def _modulation(cc, w_ada, b_ada):
    n, d = cc.shape
    cols = w_ada.shape[1]
    return pl.pallas_call(
        _mod_kernel,
        grid=(cols // COL_TILE,),
        in_specs=[
            pl.BlockSpec((n, d), lambda j: (0, 0)),
            pl.BlockSpec((d, COL_TILE), lambda j: (0, j)),
            pl.BlockSpec((1, COL_TILE), lambda j: (0, j)),
        ],
        out_specs=pl.BlockSpec((n, COL_TILE), lambda j: (0, j)),
        out_shape=jax.ShapeDtypeStruct((n, cols), F32),
        compiler_params=pltpu.CompilerParams(
            dimension_semantics=("parallel",), vmem_limit_bytes=VMEM_LIMIT),
        name="adaln_mod",
    )(cc, w_ada, b_ada.reshape(1, cols))


def _store_heads(ref, t):
    for hh in range(N_HEADS):
        ref[0, 0, hh] = t[:, hh * HEAD_W:(hh + 1) * HEAD_W].astype(ref.dtype)


def _rope_heads(ref, t, cos, sin, scale):
    for hh in range(N_HEADS):
        th = t[:, hh * HEAD_W:(hh + 1) * HEAD_W]
        rot = pltpu.roll(th, HEAD_W // 2, 1)
        ref[0, 0, hh] = ((th * cos + rot * sin) * scale).astype(ref.dtype)


def _norm_to_scratch(x_ref, mod_ref, h_ref):
    shift = mod_ref[0, 0:1, :]
    scale = mod_ref[0, 1:2, :]
    h_ref[...] = _rms_mod(x_ref[0], shift, scale).astype(BF16)


def _inproj_kernel(x_ref, mod_ref, w_ref, cos_ref, sin_ref,
                   ug_ref, gz_ref, qkv_ref, h_ref, *, n_pair, n_act):
    j = pl.program_id(2)
    half = COL_TILE // 2

    @pl.when(j == 0)
    def _():
        _norm_to_scratch(x_ref, mod_ref, h_ref)

    def proj():
        return jnp.dot(h_ref[...], w_ref[...], preferred_element_type=F32)

    @pl.when(j < n_pair // 2)
    def _():
        t = proj()
        ug_ref[0] = (t[:, half:] * t[:, :half]).astype(ug_ref.dtype)

    @pl.when((j >= n_pair // 2) & (j < n_pair))
    def _():
        t = proj()
        ug_ref[0] = (t[:, :half] * _silu(t[:, half:])).astype(ug_ref.dtype)

    @pl.when((j >= n_pair) & (j < n_pair + n_act - 1))
    def _():
        gz_ref[0] = jax.nn.sigmoid(proj()).astype(gz_ref.dtype)

    @pl.when(j == n_pair + n_act - 1)
    def _():
        gz_ref[0] = _silu(proj()).astype(gz_ref.dtype)

    @pl.when(j == n_pair + n_act)
    def _():
        _rope_heads(qkv_ref, proj(), cos_ref[...], sin_ref[...], QK_SCALE * LOG2E)

    @pl.when(j == n_pair + n_act + 1)
    def _():
        _rope_heads(qkv_ref, proj(), cos_ref[...], sin_ref[...], 1.0)

    @pl.when(j == n_pair + n_act + 2)
    def _():
        _store_heads(qkv_ref, proj())


def _inproj(x, mod, w, cos_t, sin_t, n_pair, n_act):
    b, s, d = x.shape
    tm = min(ROW_TILE, s)
    n_pe = n_pair + n_act
    nj = n_pe + 3
    half = COL_TILE // 2
    qkv_spec = pl.BlockSpec((1, 1, N_HEADS, tm, HEAD_W),
                            lambda bi, i, j: (bi, jnp.clip(j - n_pe, 0, 2), 0, i, 0))
    return pl.pallas_call(
        functools.partial(_inproj_kernel, n_pair=n_pair, n_act=n_act),
        grid=(b, s // tm, nj),
        in_specs=[
            pl.BlockSpec((1, tm, d), lambda bi, i, j: (bi, i, 0)),
            pl.BlockSpec((1, 3, d), lambda bi, i, j: (bi, 0, 0)),
            pl.BlockSpec((d, COL_TILE), lambda bi, i, j: (0, j)),
            pl.BlockSpec((tm, HEAD_W), lambda bi, i, j: (i, 0)),
            pl.BlockSpec((tm, HEAD_W), lambda bi, i, j: (i, 0)),
        ],
        out_specs=[
            pl.BlockSpec((1, tm, half), lambda bi, i, j: (bi, i, jnp.minimum(j, n_pair - 1))),
            pl.BlockSpec((1, tm, COL_TILE),
                         lambda bi, i, j: (bi, i, jnp.clip(j - n_pair, 0, n_act - 1))),
            qkv_spec,
        ],
        out_shape=[
            jax.ShapeDtypeStruct((b, s, n_pair * half), BF16),
            jax.ShapeDtypeStruct((b, s, n_act * COL_TILE), BF16),
            jax.ShapeDtypeStruct((b, 3, N_HEADS, s, HEAD_W), BF16),
        ],
        scratch_shapes=[pltpu.VMEM((tm, d), BF16)],
        compiler_params=pltpu.CompilerParams(
            dimension_semantics=("parallel", "parallel", "arbitrary"),
            vmem_limit_bytes=VMEM_LIMIT),
        name="in_proj",
    )(x, mod, w, cos_t, sin_t)


def _ctx_kernel(x_ref, mod_ref, w_ref, kv_ref, h_ref):
    @pl.when(pl.program_id(1) == 0)
    def _():
        _norm_to_scratch(x_ref, mod_ref, h_ref)

    _store_heads(kv_ref, jnp.dot(h_ref[...], w_ref[...], preferred_element_type=F32))


def _ctx_proj(ctx, mod, w, k_col):
    b, c, d = ctx.shape
    ctx_row = mod.shape[0] - 1
    return pl.pallas_call(
        _ctx_kernel,
        grid=(b, 2),
        in_specs=[
            pl.BlockSpec((1, c, d), lambda bi, j: (bi, 0, 0)),
            pl.BlockSpec((1, 3, d), lambda bi, j: (ctx_row, 0, 0)),
            pl.BlockSpec((d, COL_TILE), lambda bi, j: (0, k_col + j)),
        ],
        out_specs=pl.BlockSpec((1, 1, N_HEADS, c, HEAD_W), lambda bi, j: (bi, j, 0, 0, 0)),
        out_shape=jax.ShapeDtypeStruct((b, 2, N_HEADS, c, HEAD_W), BF16),
        scratch_shapes=[pltpu.VMEM((c, d), BF16)],
        compiler_params=pltpu.CompilerParams(
            dimension_semantics=("parallel", "arbitrary"), vmem_limit_bytes=VMEM_LIMIT),
        name="ctx_proj",
    )(ctx, mod, w)


def _lane_blocks_max(s):
    m = s[:, 0:LANES]
    for u in range(1, s.shape[1] // LANES):
        m = jnp.maximum(m, s[:, u * LANES:(u + 1) * LANES])
    return m


def _exp2_shifted(s, mb):
    blocks = [jnp.exp2(s[:, u * LANES:(u + 1) * LANES] - mb) for u in range(s.shape[1] // LANES)]
    part = blocks[0]
    for blk in blocks[1:]:
        part = part + blk
    return jnp.concatenate(blocks, axis=1).astype(BF16), part


def _diff_terms(p, f, tq):
    blocks = [p[0:tq, u * LANES:(u + 1) * LANES] - f * p[tq:2 * tq, u * LANES:(u + 1) * LANES]
              for u in range(p.shape[1] // LANES)]
    return jnp.concatenate(blocks, axis=1)


def _nt_dot(a, b):
    return lax.dot_general(a, b, (((1,), (1,)), ((), ())), preferred_element_type=F32)


def _attn_kernel(dl_ref, sub_ref, q_ref, kl_ref, kc_ref, vl_ref, vc_ref, o_ref,
                 s_ref, sc_ref, p_ref, pc_ref, l_ref, ma_ref, mb_ref, fa_ref, fb_ref,
                 ra_ref, rb_ref, lhs_ref, acc_ref, *, tq, ck, kg, lam_init):
    s_len = q_ref.shape[0]
    n_tiles = s_len // tq
    m_bufs = (ma_ref, mb_ref)
    f_bufs = ((fa_ref, ra_ref), (fb_ref, rb_ref))

    dl = dl_ref[...]
    lam = (jnp.exp(jnp.sum(dl[0:1] * dl[1:2], axis=1, keepdims=True))
           - jnp.exp(jnp.sum(dl[2:3] * dl[3:4], axis=1, keepdims=True)) + lam_init)
    sub = sub_ref[...] * (1.0 - lam_init)
    lane = lax.broadcasted_iota(jnp.int32, (1, HEAD_W), 1)
    comp0 = (lane % HEAD_DIM) < (HEAD_DIM // 2)

    n_groups = s_len // (ck * kg)

    def step(t_out, t_exp, t_qk, par):
        f_ref, r_ref = f_bufs[par]
        m_exp, m_qk = m_bufs[1 - par], m_bufs[par]
        if t_qk is not None:
            q = q_ref[pl.ds(pl.multiple_of(t_qk * tq, tq), tq), :]
            zero = jnp.zeros_like(q)
            lhs_ref[0:tq, :] = jnp.where(comp0, q, zero)
            lhs_ref[tq:2 * tq, :] = jnp.where(comp0, zero, q)

        def chunk(k_c, v_c, s_c, p_c, first):
            if t_out is not None:
                part = jnp.dot(_diff_terms(p_c[...], f_ref[...], tq), v_c[...],
                               preferred_element_type=F32)
                acc_ref[...] = part if first else acc_ref[...] + part
            if t_exp is not None:
                p, part = _exp2_shifted(s_c[...], m_exp[...])
                p_c[...] = p
                l_ref[...] = part if first else l_ref[...] + part
            if t_qk is not None:
                s = _nt_dot(lhs_ref[...], k_c[...])
                s_c[...] = s
                smax = _lane_blocks_max(s)
                m_qk[...] = smax if first else jnp.maximum(m_qk[...], smax)

        chunk(kc_ref, vc_ref, sc_ref, pc_ref, True)

        def group(g, carry):
            for j in range(kg):
                c = g * kg + j
                rows = pl.ds(pl.multiple_of(c * ck, ck), ck)
                chunk(kl_ref.at[rows], vl_ref.at[rows], s_ref.at[c], p_ref.at[c], False)
            return carry

        lax.fori_loop(0, n_groups, group, 0)
        if t_out is not None:
            o = acc_ref[...] * r_ref[...]
            o = o * lax.rsqrt(jnp.mean(o * o, axis=-1, keepdims=True) + RMS_EPS) * sub
            o_ref[0, pl.ds(pl.multiple_of(t_out * tq, tq), tq), :] = o
        if t_exp is not None:
            l = jnp.sum(l_ref[...], axis=1, keepdims=True)
            l0, l1 = l[0:tq], l[tq:2 * tq]
            f_next, r_next = f_bufs[1 - par]
            f_next[...] = jnp.broadcast_to(lam * l0 / l1, f_next.shape).astype(BF16)
            r_next[...] = jnp.broadcast_to(1.0 / l0, r_next.shape)
        if t_qk is not None:
            m_qk[...] = jnp.broadcast_to(jnp.max(m_qk[...], axis=1, keepdims=True), m_qk.shape)

    step(None, None, 0, 0)
    step(None, 0, 1, 1)

    def pair(u, carry):
        t = 2 * u
        step(t, t + 1, t + 2, 0)
        step(t + 1, t + 2, t + 3, 1)
        return carry

    lax.fori_loop(0, n_tiles // 2 - 1, pair, 0)
    step(n_tiles - 2, n_tiles - 1, None, 0)
    step(n_tiles - 1, None, None, 1)


def _attention(dl, subln, qkv, kv_ctx, lam_init):
    b, _, h, s, _ = qkv.shape
    c = kv_ctx.shape[3]
    tq = min(Q_TILE, s // 2)
    ck = min(KEY_CHUNK, s)
    kg = min(KEY_GROUP, s // ck)
    assert s % (2 * tq) == 0 and s % (ck * kg) == 0
    head = lambda n, which: pl.BlockSpec((None, None, None, n, HEAD_W),
                                         lambda bi, hi: (bi, which, hi, 0, 0))
    m_buf = pltpu.VMEM((2 * tq, LANES), F32)
    f_buf = pltpu.VMEM((tq, LANES), BF16)
    r_buf = pltpu.VMEM((tq, LANES), F32)
    return pl.pallas_call(
        functools.partial(_attn_kernel, tq=tq, ck=ck, kg=kg, lam_init=lam_init),
        grid=(b, h),
        in_specs=[
            pl.BlockSpec(dl.shape, lambda bi, hi: (0, 0)),
            pl.BlockSpec(subln.shape, lambda bi, hi: (0, 0)),
            head(s, 0), head(s, 1), head(c, 0), head(s, 2), head(c, 1),
        ],
        out_specs=pl.BlockSpec((1, s, HEAD_W), lambda bi, hi: (bi, 0, hi)),
        out_shape=jax.ShapeDtypeStruct((b, s, h * HEAD_W), F32),
        scratch_shapes=[
            pltpu.VMEM((s // ck, 2 * tq, ck), F32), pltpu.VMEM((2 * tq, c), F32),
            pltpu.VMEM((s // ck, 2 * tq, ck), BF16), pltpu.VMEM((2 * tq, c), BF16),
            m_buf, m_buf, m_buf, f_buf, f_buf, r_buf, r_buf,
            pltpu.VMEM((2 * tq, HEAD_W), BF16), r_buf,
        ],
        compiler_params=pltpu.CompilerParams(
            dimension_semantics=("parallel", "parallel"), vmem_limit_bytes=VMEM_LIMIT),
        name="diff_attn",
    )(dl, subln, qkv, qkv, kv_ctx, qkv, kv_ctx)


SUBLANES = 8


def _fold_rows(x, op):
    return op(x.reshape(x.shape[0] // SUBLANES, SUBLANES, x.shape[1]), axis=0)


def _attn_t_kernel(dl_ref, sub_ref, q_ref, kl_ref, kc_ref, vl_ref, vc_ref, o_ref,
                   vt_ref, s_ref, sc_ref, p_ref, pc_ref, ma_ref, mb_ref, fa_ref, fb_ref,
                   ra_ref, rb_ref, *, tq, ck, lam_init):
    s_len = q_ref.shape[0]
    c_len = kc_ref.shape[0]
    n_tiles = s_len // tq
    m_bufs = (ma_ref, mb_ref)
    f_bufs = ((fa_ref, ra_ref), (fb_ref, rb_ref))

    dl = dl_ref[...]
    lam = (jnp.exp(jnp.sum(dl[0:1] * dl[1:2], axis=1, keepdims=True))
           - jnp.exp(jnp.sum(dl[2:3] * dl[3:4], axis=1, keepdims=True)) + lam_init)
    sub = jnp.broadcast_to(sub_ref[...] * (1.0 - lam_init), (HEAD_W, tq))
    lane = lax.broadcasted_iota(jnp.int32, (1, HEAD_W), 1)
    comp0 = (lane % HEAD_DIM) < (HEAD_DIM // 2)

    vt_ref[:, 0:c_len] = vc_ref[...].T
    for c in range(s_len // ck):
        vt_ref[:, c_len + c * ck:c_len + (c + 1) * ck] = vl_ref[c * ck:(c + 1) * ck, :].T

    chunks = [(kc_ref, vt_ref.at[:, 0:c_len], sc_ref, pc_ref)]
    for c in range(s_len // ck):
        chunks.append((kl_ref.at[c * ck:(c + 1) * ck],
                       vt_ref.at[:, c_len + c * ck:c_len + (c + 1) * ck],
                       s_ref.at[c], p_ref.at[c]))

    def step(t_out, t_exp, t_qk, par):
        f_ref, r_ref = f_bufs[par]
        m_exp, m_qk = m_bufs[1 - par], m_bufs[par]
        if t_qk is not None:
            q = q_ref[pl.ds(pl.multiple_of(t_qk * tq, tq), tq), :]
            zero = jnp.zeros_like(q)
            lhs = jnp.concatenate([jnp.where(comp0, q, zero), jnp.where(comp0, zero, q)], axis=0)
        acc = m8 = l8 = None
        for k_c, vt_c, s_c, p_c in chunks:
            if t_out is not None:
                p = p_c[...]
                n = p.shape[0]
                p0 = p[:, 0:tq].reshape(n // BF16_ROWS, BF16_ROWS, tq)
                p1 = p[:, tq:2 * tq].reshape(n // BF16_ROWS, BF16_ROWS, tq)
                a = (p0 - f_ref[...] * p1).reshape(n, tq)
                part = jnp.dot(vt_c[...], a, preferred_element_type=F32)
                acc = part if acc is None else acc + part
            if t_exp is not None:
                s = s_c[...]
                n = s.shape[0]
                p = jnp.exp2(s.reshape(n // SUBLANES, SUBLANES, 2 * tq) - m_exp[...])
                p_c[...] = p.reshape(n, 2 * tq).astype(BF16)
                part = jnp.sum(p, axis=0)
                l8 = part if l8 is None else l8 + part
            if t_qk is not None:
                s = _nt_dot(k_c[...], lhs)
                s_c[...] = s
                part = _fold_rows(s, jnp.max)
                m8 = part if m8 is None else jnp.maximum(m8, part)
        if t_out is not None:
            o = (acc.reshape(HEAD_W // SUBLANES, SUBLANES, tq) * r_ref[...]).reshape(HEAD_W, tq)
            o = o * lax.rsqrt(jnp.mean(o * o, axis=0, keepdims=True) + RMS_EPS) * sub
            o_ref[0, pl.ds(pl.multiple_of(t_out * tq, tq), tq), :] = o.T
        if t_exp is not None:
            l = jnp.sum(l8, axis=0, keepdims=True)
            l0, l1 = l[:, 0:tq], l[:, tq:2 * tq]
            f_next, r_next = f_bufs[1 - par]
            f_next[...] = jnp.broadcast_to(lam * l0 / l1, f_next.shape).astype(BF16)
            r_next[...] = jnp.broadcast_to(1.0 / l0, r_next.shape)
        if t_qk is not None:
            m_qk[...] = jnp.broadcast_to(jnp.max(m8, axis=0, keepdims=True), m_qk.shape)

    step(None, None, 0, 0)
    step(None, 0, 1, 1)

    def pair(u, carry):
        t = 2 * u
        step(t, t + 1, t + 2, 0)
        step(t + 1, t + 2, t + 3, 1)
        return carry

    lax.fori_loop(0, n_tiles // 2 - 1, pair, 0)
    step(n_tiles - 2, n_tiles - 1, None, 0)
    step(n_tiles - 1, None, None, 1)


def _attention_t(dl, subln, qkv, kv_ctx, lam_init):
    b, _, h, s, _ = qkv.shape
    c = kv_ctx.shape[3]
    tq = min(Q_TILE, s // 2)
    ck = min(KEY_CHUNK, s)
    assert s % (2 * tq) == 0 and s % ck == 0
    head = lambda n, which: pl.BlockSpec((None, None, None, n, HEAD_W),
                                         lambda bi, hi: (bi, which, hi, 0, 0))
    m_buf = pltpu.VMEM((SUBLANES, 2 * tq), F32)
    f_buf = pltpu.VMEM((BF16_ROWS, tq), BF16)
    r_buf = pltpu.VMEM((SUBLANES, tq), F32)
    return pl.pallas_call(
        functools.partial(_attn_t_kernel, tq=tq, ck=ck, lam_init=lam_init),
        grid=(b, h),
        in_specs=[
            pl.BlockSpec(dl.shape, lambda bi, hi: (0, 0)),
            pl.BlockSpec(subln.shape, lambda bi, hi: (0, 0)),
            head(s, 0), head(s, 1), head(c, 0), head(s, 2), head(c, 1),
        ],
        out_specs=pl.BlockSpec((1, s, HEAD_W), lambda bi, hi: (bi, 0, hi)),
        out_shape=jax.ShapeDtypeStruct((b, s, h * HEAD_W), F32),
        scratch_shapes=[
            pltpu.VMEM((HEAD_W, s + c), BF16),
            pltpu.VMEM((s // ck, ck, 2 * tq), F32), pltpu.VMEM((c, 2 * tq), F32),
            pltpu.VMEM((s // ck, ck, 2 * tq), BF16), pltpu.VMEM((c, 2 * tq), BF16),
            m_buf, m_buf, f_buf, f_buf, r_buf, r_buf,
        ],
        compiler_params=pltpu.CompilerParams(
            dimension_semantics=("parallel", "parallel"), vmem_limit_bytes=VMEM_LIMIT),
        name="diff_attn",
    )(dl, subln, qkv, qkv, kv_ctx, qkv, kv_ctx)


BF16_ROWS = 16


def _out_kernel(u_ref, g2_ref, up_ref, un_ref, sig_ref, szb_ref, o_ref, x_ref, mod_ref, cw_ref,
                woa_ref, wob_ref, wo_ref, fw_ref, out_ref, *, tm):
    i = pl.program_id(1)
    last = pl.num_programs(1) - 1
    d = x_ref.shape[2]

    u = u_ref[0].astype(F32)
    u_prev = jnp.where(i > 0, up_ref[0, BF16_ROWS - 1:BF16_ROWS, :].astype(F32), 0.0)
    u_next = jnp.where(i < last, un_ref[0, 0:1, :].astype(F32), 0.0)
    row = lax.broadcasted_iota(jnp.int32, (tm, 1), 0)
    u_dn = jnp.where(row == 0, u_prev, pltpu.roll(u, 1, 0))
    u_up = jnp.where(row == tm - 1, u_next, pltpu.roll(u, tm - 1, 0))
    y = cw_ref[0:1, :] * u_dn + cw_ref[1:2, :] * u + cw_ref[2:3, :] * u_up
    a_in = (g2_ref[0].astype(F32) * y).astype(BF16)
    y_a = jnp.dot(a_in, woa_ref[...], preferred_element_type=F32)

    b_in = (o_ref[0] * szb_ref[0].astype(F32)).astype(BF16)
    y_b = jnp.dot(b_in, wob_ref[...], preferred_element_type=F32)

    merged = (sig_ref[0, :, 0:d].astype(F32) * y_a
              + sig_ref[0, :, d:2 * d].astype(F32) * y_b).astype(BF16)
    y_o = jnp.dot(merged, wo_ref[...], preferred_element_type=F32)

    xo = x_ref[0] + mod_ref[0, 2:3, :] * y_o
    out_ref[0] = xo * lax.rsqrt(jnp.mean(xo * xo, axis=-1, keepdims=True) + RMS_EPS) * fw_ref[...]


def _out_stage(ug, gz, o, x, mod, conv_w, woa, wob, wo, fw):
    b, s, d = x.shape
    cw = conv_w.shape[-1]
    tm = min(OUT_TILE, s)
    tb = tm // BF16_ROWS
    nblk = s // BF16_ROWS
    col = lambda cb: (lambda bi, i: (bi, i, cb))
    const2 = lambda bi, i: (0, 0)
    resident = dict(pipeline_mode=pl.Buffered(1))
    return pl.pallas_call(
        functools.partial(_out_kernel, tm=tm),
        grid=(b, s // tm),
        in_specs=[
            pl.BlockSpec((1, tm, cw), col(0)),
            pl.BlockSpec((1, tm, cw), col(1)),
            pl.BlockSpec((1, BF16_ROWS, cw), lambda bi, i: (bi, jnp.maximum(i * tb - 1, 0), 0)),
            pl.BlockSpec((1, BF16_ROWS, cw),
                         lambda bi, i: (bi, jnp.minimum((i + 1) * tb, nblk - 1), 0)),
            pl.BlockSpec((1, tm, 2 * d), col(0)),
            pl.BlockSpec((1, tm, COL_TILE), col(2 * d // COL_TILE)),
            pl.BlockSpec((1, tm, COL_TILE), col(0)),
            pl.BlockSpec((1, tm, d), col(0)),
            pl.BlockSpec((1, 3, d), lambda bi, i: (bi, 0, 0)),
            pl.BlockSpec(conv_w.shape, const2),
            pl.BlockSpec(woa.shape, const2, **resident),
            pl.BlockSpec(wob.shape, const2, **resident),
            pl.BlockSpec(wo.shape, const2, **resident),
            pl.BlockSpec(fw.shape, const2),
        ],
        out_specs=pl.BlockSpec((1, tm, d), col(0)),
        out_shape=jax.ShapeDtypeStruct((b, s, d), x.dtype),
        compiler_params=pltpu.CompilerParams(
            dimension_semantics=("parallel", "parallel"), vmem_limit_bytes=VMEM_LIMIT),
        name="out_stage",
    )(ug, ug, ug, ug, gz, gz, o, x, mod, conv_w, woa, wob, wo, fw)


def _head_lane_order(w):
    lead = w.shape[:-1]
    w = w.reshape(lead + (N_HEADS, 2, 2, 2, AXIS_ROT // 2))
    n = len(lead)
    w = jnp.transpose(w, tuple(range(n)) + (n, n + 3, n + 1, n + 2, n + 4))
    return w.reshape(lead + (N_HEADS * HEAD_W,))


def _rope_tables(n_tokens):
    rows = n_tokens // GRID_W
    row = jnp.repeat(jnp.arange(rows, dtype=F32), GRID_W)
    col = jnp.tile(jnp.arange(GRID_W, dtype=F32), rows)
    inv_freq = ROPE_THETA ** (-jnp.arange(0, AXIS_ROT, 2, dtype=F32) / AXIS_ROT)
    ang = jnp.concatenate([row[:, None] * inv_freq, col[:, None] * inv_freq], axis=-1)
    ang = jnp.tile(ang, (1, 4))
    sign = jnp.where(jnp.arange(HEAD_W) < HEAD_W // 2, -1.0, 1.0).astype(F32)
    return jnp.cos(ang), jnp.sin(ang) * sign


def kernel(x, c, ctx, c_ctx, w_ada, b_ada, w_in, conv_w, diff_lambda, subln_w,
           w_out_a, w_out_b, w_o, final_norm_w):
    b, s, d = x.shape
    cw = conv_w.shape[-1]
    aw = w_out_b.shape[1]
    assert w_in.shape[0] == 1, "single-layer block"
    assert cw == COL_TILE and aw == COL_TILE == N_HEADS * HEAD_W and d == 2 * COL_TILE
    assert s % GRID_W == 0 and s % min(ROW_TILE, s) == 0 and ctx.shape[1] % LANES == 0
    lam_init = 0.8 - 0.6 * math.exp(-0.3 * 0)

    wi = w_in[0]
    half = COL_TILE // 2
    a_end = 4 * cw
    wxa, wgb, wgc, wza = (wi[:, n * cw:(n + 1) * cw] for n in range(4))
    wq, wk, wv, wzb = (wi[:, a_end + n * aw:a_end + (n + 1) * aw] for n in range(4))
    wg = wi[:, a_end + 4 * aw:]
    pairs = []
    for first, second in ((wxa, wgc), (wgb, wza)):
        for n in range(cw // half):
            pairs += [first[:, n * half:(n + 1) * half], second[:, n * half:(n + 1) * half]]
    w = jnp.concatenate(
        pairs + [wg, wzb, _head_lane_order(wq), _head_lane_order(wk), wv], axis=1).astype(BF16)
    n_pair = 4 * cw // COL_TILE
    n_act = (2 * d + aw) // COL_TILE

    cc = jnp.concatenate([c, c_ctx[None, :]], axis=0)
    mod = _modulation(cc, w_ada[0], b_ada[0]).reshape(b + 1, 3, d)

    cos_t, sin_t = _rope_tables(s)
    ug, gz, qkv = _inproj(x, mod, w, cos_t, sin_t, n_pair, n_act)
    kv_ctx = _ctx_proj(ctx, mod, w, n_pair + n_act + 1)

    o = _attention_t(diff_lambda[0], subln_w[0].reshape(HEAD_W, 1), qkv, kv_ctx, lam_init)

    return _out_stage(ug, gz, o, x, mod, conv_w[0],
                      w_out_a[0].astype(BF16), w_out_b[0].astype(BF16), w_o[0].astype(BF16),
                      final_norm_w.reshape(1, d))
```

```python
import functools
import math

import jax
import jax.numpy as jnp
from jax import lax
from jax.experimental import pallas as pl
from jax.experimental.pallas import tpu as pltpu

F32 = jnp.float32
BF16 = jnp.bfloat16

N_HEADS = 8
HEAD_DIM = 64
HEAD_W = 2 * HEAD_DIM
GRID_W = 64
ROPE_THETA = 10000.0
AXIS_ROT = HEAD_DIM // 2
RMS_EPS = 1e-6
QK_SCALE = HEAD_DIM ** -0.5
LOG2E = math.log2(math.e)
LANES = 128
SUBLANES = 8
BF16_ROWS = 16
COL_TILE = 1024

ROW_TILE = 1024
Q_TILE = 256
KEY_CHUNK = 512
HEAD_GROUP = 2
OUT_TILE = 512
VMEM_LIMIT = 56 * 1024 * 1024


def _silu(x):
    return x * jax.nn.sigmoid(x)


def _rms_mod(x, shift, scale):
    ms = jnp.mean(x * x, axis=-1, keepdims=True)
    return (x * lax.rsqrt(ms + RMS_EPS)) * (1.0 + scale) + shift


def _head_cols(t, hh):
    return t[:, hh * HEAD_W:(hh + 1) * HEAD_W]


def _mod_kernel(c_ref, w_ref, b_ref, o_ref):
    a = _silu(c_ref[...]).astype(BF16)
    o_ref[...] = jnp.dot(a, w_ref[...].astype(BF16), preferred_element_type=F32) + b_ref[...]


def _modulation(cc, w_ada, b_ada):
    n, d = cc.shape
    cols = w_ada.shape[1]
    return pl.pallas_call(
        _mod_kernel,
        grid=(cols // COL_TILE,),
        in_specs=[
            pl.BlockSpec((n, d), lambda j: (0, 0)),
            pl.BlockSpec((d, COL_TILE), lambda j: (0, j)),
            pl.BlockSpec((1, COL_TILE), lambda j: (0, j)),
        ],
        out_specs=pl.BlockSpec((n, COL_TILE), lambda j: (0, j)),
        out_shape=jax.ShapeDtypeStruct((n, cols), F32),
        compiler_params=pltpu.CompilerParams(
            dimension_semantics=("parallel",), vmem_limit_bytes=VMEM_LIMIT),
        name="adaln_mod",
    )(cc, w_ada, b_ada.reshape(1, cols))


def _rope_heads(ref, t, cos, sin, scale):
    for hh in range(N_HEADS):
        th = _head_cols(t, hh)
        rot = pltpu.roll(th, HEAD_W // 2, 1)
        ref[0, 0, hh] = ((th * cos + rot * sin) * scale).astype(ref.dtype)


def _store_heads_transposed(ref, t):
    for hh in range(N_HEADS):
        ref[0, hh] = _head_cols(t, hh).T.astype(ref.dtype)


def _norm_to_scratch(x_ref, mod_ref, h_ref):
    shift = mod_ref[0, 0:1, :]
    scale = mod_ref[0, 1:2, :]
    h_ref[...] = _rms_mod(x_ref[0], shift, scale).astype(BF16)


def _inproj_kernel(x_ref, mod_ref, w_ref, cos_ref, sin_ref,
                   ug_ref, gz_ref, qk_ref, vt_ref, h_ref, t_ref, *, n_pair, n_act):
    j = pl.program_id(2)
    half = COL_TILE // 2
    n_pe = n_pair + n_act

    def project():
        t_ref[...] = jnp.dot(h_ref[...], w_ref[...], preferred_element_type=F32)

    def previous():
        return t_ref[...]

    @pl.when(j == 0)
    def _():
        _norm_to_scratch(x_ref, mod_ref, h_ref)
        project()

    @pl.when((j >= 1) & (j <= n_pair // 2))
    def _():
        t = previous()
        ug_ref[0] = (t[:, half:] * t[:, :half]).astype(ug_ref.dtype)
        project()

    @pl.when((j > n_pair // 2) & (j <= n_pair))
    def _():
        t = previous()
        ug_ref[0] = (t[:, :half] * _silu(t[:, half:])).astype(ug_ref.dtype)
        project()

    @pl.when((j > n_pair) & (j < n_pe))
    def _():
        gz_ref[0] = jax.nn.sigmoid(previous()).astype(gz_ref.dtype)
        project()

    @pl.when(j == n_pe)
    def _():
        gz_ref[0] = _silu(previous()).astype(gz_ref.dtype)
        project()

    @pl.when(j == n_pe + 1)
    def _():
        _rope_heads(qk_ref, previous(), cos_ref[...], sin_ref[...], QK_SCALE * LOG2E)
        project()

    @pl.when(j == n_pe + 2)
    def _():
        _rope_heads(qk_ref, previous(), cos_ref[...], sin_ref[...], 1.0)
        _store_heads_transposed(
            vt_ref, jnp.dot(h_ref[...], w_ref[...], preferred_element_type=F32))


def _inproj(x, mod, w, cos_t, sin_t, n_pair, n_act):
    b, s, d = x.shape
    tm = min(ROW_TILE, s)
    n_pe = n_pair + n_act
    half = COL_TILE // 2
    return pl.pallas_call(
        functools.partial(_inproj_kernel, n_pair=n_pair, n_act=n_act),
        grid=(b, s // tm, n_pe + 3),
        in_specs=[
            pl.BlockSpec((1, tm, d), lambda bi, i, j: (bi, i, 0)),
            pl.BlockSpec((1, 3, d), lambda bi, i, j: (bi, 0, 0)),
            pl.BlockSpec((d, COL_TILE), lambda bi, i, j: (0, j)),
            pl.BlockSpec((tm, HEAD_W), lambda bi, i, j: (i, 0)),
            pl.BlockSpec((tm, HEAD_W), lambda bi, i, j: (i, 0)),
        ],
        out_specs=[
            pl.BlockSpec((1, tm, half), lambda bi, i, j: (bi, i, jnp.clip(j - 1, 0, n_pair - 1))),
            pl.BlockSpec((1, tm, COL_TILE),
                         lambda bi, i, j: (bi, i, jnp.clip(j - 1 - n_pair, 0, n_act - 1))),
            pl.BlockSpec((1, 1, N_HEADS, tm, HEAD_W),
                         lambda bi, i, j: (bi, jnp.clip(j - 1 - n_pe, 0, 1), 0, i, 0)),
            pl.BlockSpec((1, N_HEADS, HEAD_W, tm), lambda bi, i, j: (bi, 0, 0, i)),
        ],
        out_shape=[
            jax.ShapeDtypeStruct((b, s, n_pair * half), BF16),
            jax.ShapeDtypeStruct((b, s, n_act * COL_TILE), BF16),
            jax.ShapeDtypeStruct((b, 2, N_HEADS, s, HEAD_W), BF16),
            jax.ShapeDtypeStruct((b, N_HEADS, HEAD_W, s), BF16),
        ],
        scratch_shapes=[pltpu.VMEM((tm, d), BF16), pltpu.VMEM((tm, COL_TILE), F32)],
        compiler_params=pltpu.CompilerParams(
            dimension_semantics=("parallel", "parallel", "arbitrary"),
            vmem_limit_bytes=VMEM_LIMIT),
        name="in_proj",
    )(x, mod, w, cos_t, sin_t)


def _ctx_kernel(x_ref, mod_ref, w_ref, k_ref, vt_ref, h_ref):
    j = pl.program_id(1)

    @pl.when(j == 0)
    def _():
        _norm_to_scratch(x_ref, mod_ref, h_ref)

    def proj():
        return jnp.dot(h_ref[...], w_ref[...], preferred_element_type=F32)

    @pl.when(j == 0)
    def _():
        t = proj()
        for hh in range(N_HEADS):
            k_ref[0, hh] = _head_cols(t, hh).astype(k_ref.dtype)

    @pl.when(j == 1)
    def _():
        _store_heads_transposed(vt_ref, proj())


def _ctx_proj(ctx, mod, w, k_col):
    b, c, d = ctx.shape
    ctx_row = mod.shape[0] - 1
    return pl.pallas_call(
        _ctx_kernel,
        grid=(b, 2),
        in_specs=[
            pl.BlockSpec((1, c, d), lambda bi, j: (bi, 0, 0)),
            pl.BlockSpec((1, 3, d), lambda bi, j: (ctx_row, 0, 0)),
            pl.BlockSpec((d, COL_TILE), lambda bi, j: (0, k_col + j)),
        ],
        out_specs=[
            pl.BlockSpec((1, N_HEADS, c, HEAD_W), lambda bi, j: (bi, 0, 0, 0)),
            pl.BlockSpec((1, N_HEADS, HEAD_W, c), lambda bi, j: (bi, 0, 0, 0)),
        ],
        out_shape=[
            jax.ShapeDtypeStruct((b, N_HEADS, c, HEAD_W), BF16),
            jax.ShapeDtypeStruct((b, N_HEADS, HEAD_W, c), BF16),
        ],
        scratch_shapes=[pltpu.VMEM((c, d), BF16)],
        compiler_params=pltpu.CompilerParams(
            dimension_semantics=("parallel", "arbitrary"), vmem_limit_bytes=VMEM_LIMIT),
        name="ctx_proj",
    )(ctx, mod, w)


def _nt_dot(a, b):
    return lax.dot_general(a, b, (((1,), (1,)), ((), ())), preferred_element_type=F32)


def _fold_rows(x, op):
    return op(x.reshape(x.shape[0] // SUBLANES, SUBLANES, x.shape[1]), axis=0)


def _attn_kernel(dl_ref, sub_ref, q_ref, kl_ref, kc_ref, vtl_ref, vtc_ref, o_ref,
                 s_ref, sc_ref, p_ref, pc_ref, ma_ref, mb_ref, fa_ref, fb_ref, ra_ref, rb_ref,
                 *, tq, ck, lam_init):
    n_heads, s_len, _ = q_ref.shape
    n_tiles = s_len // tq
    tile_bits = n_tiles.bit_length() - 1
    n_steps = n_heads * n_tiles
    m_bufs = (ma_ref, mb_ref)
    f_bufs = ((fa_ref, ra_ref), (fb_ref, rb_ref))

    dl = dl_ref[...]
    lam = (jnp.exp(jnp.sum(dl[0:1] * dl[1:2], axis=1, keepdims=True))
           - jnp.exp(jnp.sum(dl[2:3] * dl[3:4], axis=1, keepdims=True)) + lam_init)
    sub = jnp.broadcast_to(sub_ref[...] * (1.0 - lam_init), (HEAD_W, tq))
    lane = lax.broadcasted_iota(jnp.int32, (1, HEAD_W), 1)
    comp0 = (lane % HEAD_DIM) < (HEAD_DIM // 2)

    def head_tile(t):
        if isinstance(t, int):
            return divmod(t, n_tiles)
        return lax.shift_right_logical(t, tile_bits), lax.bitwise_and(t, n_tiles - 1)

    def step(t_out, t_exp, t_qk, par):
        f_ref, r_ref = f_bufs[par]
        m_exp, m_qk = m_bufs[1 - par], m_bufs[par]
        if t_out is not None:
            h_out, i_out = head_tile(t_out)
        if t_qk is not None:
            h_qk, i_qk = head_tile(t_qk)
            q = q_ref[h_qk, pl.ds(pl.multiple_of(i_qk * tq, tq), tq), :]
            zero = jnp.zeros_like(q)
            lhs = jnp.concatenate([jnp.where(comp0, q, zero), jnp.where(comp0, zero, q)], axis=0)
        acc = m8 = l8 = None
        for c in range(-1, s_len // ck):
            rows = slice(None) if c < 0 else slice(c * ck, (c + 1) * ck)
            s_c, p_c = (sc_ref, pc_ref) if c < 0 else (s_ref.at[c], p_ref.at[c])
            if t_out is not None:
                vt = vtc_ref[h_out] if c < 0 else vtl_ref[h_out, :, rows]
                p = p_c[...]
                n = p.shape[0]
                p0 = p[:, 0:tq].reshape(n // BF16_ROWS, BF16_ROWS, tq)
                p1 = p[:, tq:2 * tq].reshape(n // BF16_ROWS, BF16_ROWS, tq)
                a = (p0 - f_ref[...] * p1).reshape(n, tq)
                part = jnp.dot(vt, a, preferred_element_type=F32)
                acc = part if acc is None else acc + part
            if t_exp is not None:
                s = s_c[...]
                n = s.shape[0]
                p = jnp.exp2(s.reshape(n // SUBLANES, SUBLANES, 2 * tq) - m_exp[...])
                p_c[...] = p.reshape(n, 2 * tq).astype(BF16)
                part = jnp.sum(p, axis=0)
                l8 = part if l8 is None else l8 + part
            if t_qk is not None:
                keys = kc_ref[h_qk] if c < 0 else kl_ref[h_qk, rows, :]
                s = _nt_dot(keys, lhs)
                s_c[...] = s
                part = _fold_rows(s, jnp.max)
                m8 = part if m8 is None else jnp.maximum(m8, part)
        if t_out is not None:
            o = (acc.reshape(HEAD_W // SUBLANES, SUBLANES, tq) * r_ref[...]).reshape(HEAD_W, tq)
            o = o * lax.rsqrt(jnp.mean(o * o, axis=0, keepdims=True) + RMS_EPS) * sub
            o_ref[0, h_out, pl.ds(pl.multiple_of(i_out * tq, tq), tq), :] = o.T.astype(o_ref.dtype)
        if t_exp is not None:
            l = jnp.sum(l8, axis=0, keepdims=True)
            l0, l1 = l[:, 0:tq], l[:, tq:2 * tq]
            f_next, r_next = f_bufs[1 - par]
            f_next[...] = jnp.broadcast_to(lam * l0 / l1, f_next.shape).astype(BF16)
            r_next[...] = jnp.broadcast_to(1.0 / l0, r_next.shape)
        if t_qk is not None:
            m_qk[...] = jnp.broadcast_to(jnp.max(m8, axis=0, keepdims=True), m_qk.shape)

    step(None, None, 0, 0)
    step(None, 0, 1, 1)

    def pair(u, carry):
        t = 2 * u
        step(t, t + 1, t + 2, 0)
        step(t + 1, t + 2, t + 3, 1)
        return carry

    lax.fori_loop(0, n_steps // 2 - 1, pair, 0)
    step(n_steps - 2, n_steps - 1, None, 0)
    step(n_steps - 1, None, None, 1)


def _attention(dl, subln, qk, vt, k_ctx, vt_ctx, lam_init):
    b, _, h, s, _ = qk.shape
    c = k_ctx.shape[2]
    hg = min(HEAD_GROUP, h)
    tq = min(Q_TILE, s // 2)
    ck = min(KEY_CHUNK, s)
    n_tiles = s // tq
    assert s % (2 * tq) == 0 and s % ck == 0 and h % hg == 0 and n_tiles & (n_tiles - 1) == 0
    m_buf = pltpu.VMEM((SUBLANES, 2 * tq), F32)
    f_buf = pltpu.VMEM((BF16_ROWS, tq), BF16)
    r_buf = pltpu.VMEM((SUBLANES, tq), F32)
    return pl.pallas_call(
        functools.partial(_attn_kernel, tq=tq, ck=ck, lam_init=lam_init),
        grid=(b, h // hg),
        in_specs=[
            pl.BlockSpec(dl.shape, lambda bi, gi: (0, 0)),
            pl.BlockSpec(subln.shape, lambda bi, gi: (0, 0)),
            pl.BlockSpec((None, None, hg, s, HEAD_W), lambda bi, gi: (bi, 0, gi, 0, 0)),
            pl.BlockSpec((None, None, hg, s, HEAD_W), lambda bi, gi: (bi, 1, gi, 0, 0)),
            pl.BlockSpec((None, hg, c, HEAD_W), lambda bi, gi: (bi, gi, 0, 0)),
            pl.BlockSpec((None, hg, HEAD_W, s), lambda bi, gi: (bi, gi, 0, 0)),
            pl.BlockSpec((None, hg, HEAD_W, c), lambda bi, gi: (bi, gi, 0, 0)),
        ],
        out_specs=pl.BlockSpec((1, hg, s, HEAD_W), lambda bi, gi: (bi, gi, 0, 0)),
        out_shape=jax.ShapeDtypeStruct((b, h, s, HEAD_W), F32),
        scratch_shapes=[
            pltpu.VMEM((s // ck, ck, 2 * tq), F32), pltpu.VMEM((c, 2 * tq), F32),
            pltpu.VMEM((s // ck, ck, 2 * tq), BF16), pltpu.VMEM((c, 2 * tq), BF16),
            m_buf, m_buf, f_buf, f_buf, r_buf, r_buf,
        ],
        compiler_params=pltpu.CompilerParams(
            dimension_semantics=("parallel", "parallel"), vmem_limit_bytes=VMEM_LIMIT),
        name="diff_attn",
    )(dl, subln, qk, qk, k_ctx, vt, vt_ctx)


def _out_kernel(u_ref, g2_ref, up_ref, un_ref, sig_ref, szb_ref, o_ref, x_ref, mod_ref, cw_ref,
                woa_ref, wob_ref, wo_ref, fw_ref, out_ref, *, tm):
    i = pl.program_id(1)
    last = pl.num_programs(1) - 1
    d = x_ref.shape[2]

    u = u_ref[0].astype(F32)
    u_prev = jnp.where(i > 0, up_ref[0, BF16_ROWS - 1:BF16_ROWS, :].astype(F32), 0.0)
    u_next = jnp.where(i < last, un_ref[0, 0:1, :].astype(F32), 0.0)
    row = lax.broadcasted_iota(jnp.int32, (tm, 1), 0)
    u_dn = jnp.where(row == 0, u_prev, pltpu.roll(u, 1, 0))
    u_up = jnp.where(row == tm - 1, u_next, pltpu.roll(u, tm - 1, 0))
    y = cw_ref[0:1, :] * u_dn + cw_ref[1:2, :] * u + cw_ref[2:3, :] * u_up
    a_in = (g2_ref[0].astype(F32) * y).astype(BF16)
    y_a = jnp.dot(a_in, woa_ref[...], preferred_element_type=F32)

    o = jnp.concatenate([o_ref[0, hh] for hh in range(N_HEADS)], axis=1)
    b_in = (o * szb_ref[0].astype(F32)).astype(BF16)
    y_b = jnp.dot(b_in, wob_ref[...], preferred_element_type=F32)

    merged = (sig_ref[0, :, 0:d].astype(F32) * y_a
              + sig_ref[0, :, d:2 * d].astype(F32) * y_b).astype(BF16)
    y_o = jnp.dot(merged, wo_ref[...], preferred_element_type=F32)

    xo = x_ref[0] + mod_ref[0, 2:3, :] * y_o
    out_ref[0] = xo * lax.rsqrt(jnp.mean(xo * xo, axis=-1, keepdims=True) + RMS_EPS) * fw_ref[...]


def _out_stage(ug, gz, o, x, mod, conv_w, woa, wob, wo, fw):
    b, s, d = x.shape
    cw = conv_w.shape[-1]
    tm = min(OUT_TILE, s)
    tb = tm // BF16_ROWS
    nblk = s // BF16_ROWS
    col = lambda cb: (lambda bi, i: (bi, i, cb))
    const2 = lambda bi, i: (0, 0)
    resident = dict(pipeline_mode=pl.Buffered(1))
    return pl.pallas_call(
        functools.partial(_out_kernel, tm=tm),
        grid=(b, s // tm),
        in_specs=[
            pl.BlockSpec((1, tm, cw), col(0)),
            pl.BlockSpec((1, tm, cw), col(1)),
            pl.BlockSpec((1, BF16_ROWS, cw), lambda bi, i: (bi, jnp.maximum(i * tb - 1, 0), 0)),
            pl.BlockSpec((1, BF16_ROWS, cw),
                         lambda bi, i: (bi, jnp.minimum((i + 1) * tb, nblk - 1), 0)),
            pl.BlockSpec((1, tm, 2 * d), col(0)),
            pl.BlockSpec((1, tm, COL_TILE), col(2 * d // COL_TILE)),
            pl.BlockSpec((1, N_HEADS, tm, HEAD_W), lambda bi, i: (bi, 0, i, 0)),
            pl.BlockSpec((1, tm, d), col(0)),
            pl.BlockSpec((1, 3, d), lambda bi, i: (bi, 0, 0)),
            pl.BlockSpec(conv_w.shape, const2),
            pl.BlockSpec(woa.shape, const2, **resident),
            pl.BlockSpec(wob.shape, const2, **resident),
            pl.BlockSpec(wo.shape, const2, **resident),
            pl.BlockSpec(fw.shape, const2),
        ],
        out_specs=pl.BlockSpec((1, tm, d), col(0)),
        out_shape=jax.ShapeDtypeStruct((b, s, d), x.dtype),
        compiler_params=pltpu.CompilerParams(
            dimension_semantics=("parallel", "parallel"), vmem_limit_bytes=VMEM_LIMIT),
        name="out_stage",
    )(ug, ug, ug, ug, gz, gz, o, x, mod, conv_w, woa, wob, wo, fw)


def _head_lane_order(w):
    lead = w.shape[:-1]
    w = w.reshape(lead + (N_HEADS, 2, 2, 2, AXIS_ROT // 2))
    n = len(lead)
    w = jnp.transpose(w, tuple(range(n)) + (n, n + 3, n + 1, n + 2, n + 4))
    return w.reshape(lead + (N_HEADS * HEAD_W,))


def _rope_tables(n_tokens):
    rows = n_tokens // GRID_W
    row = jnp.repeat(jnp.arange(rows, dtype=F32), GRID_W)
    col = jnp.tile(jnp.arange(GRID_W, dtype=F32), rows)
    inv_freq = ROPE_THETA ** (-jnp.arange(0, AXIS_ROT, 2, dtype=F32) / AXIS_ROT)
    ang = jnp.concatenate([row[:, None] * inv_freq, col[:, None] * inv_freq], axis=-1)
    ang = jnp.tile(ang, (1, 4))
    sign = jnp.where(jnp.arange(HEAD_W) < HEAD_W // 2, -1.0, 1.0).astype(F32)
    return jnp.cos(ang), jnp.sin(ang) * sign


def kernel(x, c, ctx, c_ctx, w_ada, b_ada, w_in, conv_w, diff_lambda, subln_w,
           w_out_a, w_out_b, w_o, final_norm_w):
    b, s, d = x.shape
    cw = conv_w.shape[-1]
    aw = w_out_b.shape[1]
    assert w_in.shape[0] == 1, "single-layer block"
    assert cw == COL_TILE and aw == COL_TILE == N_HEADS * HEAD_W and d == 2 * COL_TILE
    assert s % GRID_W == 0 and s % min(ROW_TILE, s) == 0 and ctx.shape[1] % LANES == 0
    lam_init = 0.8 - 0.6 * math.exp(-0.3 * 0)

    wi = w_in[0]
    half = COL_TILE // 2
    a_end = 4 * cw
    wxa, wgb, wgc, wza = (wi[:, n * cw:(n + 1) * cw] for n in range(4))
    wq, wk, wv, wzb = (wi[:, a_end + n * aw:a_end + (n + 1) * aw] for n in range(4))
    wg = wi[:, a_end + 4 * aw:]
    pairs = []
    for first, second in ((wxa, wgc), (wgb, wza)):
        for n in range(cw // half):
            pairs += [first[:, n * half:(n + 1) * half], second[:, n * half:(n + 1) * half]]
    w = jnp.concatenate(
        pairs + [wg, wzb, _head_lane_order(wq), _head_lane_order(wk), wv], axis=1).astype(BF16)
    n_pair = 4 * cw // COL_TILE
    n_act = (2 * d + aw) // COL_TILE

    cc = jnp.concatenate([c, c_ctx[None, :]], axis=0)
    mod = _modulation(cc, w_ada[0], b_ada[0]).reshape(b + 1, 3, d)

    cos_t, sin_t = _rope_tables(s)
    ug, gz, qk, vt = _inproj(x, mod, w, cos_t, sin_t, n_pair, n_act)
    k_ctx, vt_ctx = _ctx_proj(ctx, mod, w, n_pair + n_act + 1)

    o = _attention(diff_lambda[0], subln_w[0].reshape(HEAD_W, 1), qk, vt, k_ctx, vt_ctx, lam_init)

    return _out_stage(ug, gz, o, x, mod, conv_w[0],
                      w_out_a[0].astype(BF16), w_out_b[0].astype(BF16), w_o[0].astype(BF16),
                      final_norm_w.reshape(1, d))
```

```python
import functools
import math

import jax
import jax.numpy as jnp
from jax import lax
from jax.experimental import pallas as pl
from jax.experimental.pallas import tpu as pltpu

F32 = jnp.float32
BF16 = jnp.bfloat16

N_HEADS = 8
HEAD_DIM = 64
HEAD_W = 2 * HEAD_DIM
GRID_W = 64
ROPE_THETA = 10000.0
AXIS_ROT = HEAD_DIM // 2
RMS_EPS = 1e-6
QK_SCALE = HEAD_DIM ** -0.5
LOG2E = math.log2(math.e)
LANES = 128
SUBLANES = 8
BF16_ROWS = 16
COL_TILE = 1024

ROW_TILE = 1024
Q_TILE = 256
KEY_CHUNK = 512
HEAD_GROUP = 2
OUT_TILE = 512
VMEM_LIMIT = 56 * 1024 * 1024


def _silu(x):
    return x * jax.nn.sigmoid(x)


def _rms_mod(x, shift, scale):
    ms = jnp.mean(x * x, axis=-1, keepdims=True)
    return (x * lax.rsqrt(ms + RMS_EPS)) * (1.0 + scale) + shift


def _head_cols(t, hh):
    return t[:, hh * HEAD_W:(hh + 1) * HEAD_W]


def _mod_kernel(c_ref, w_ref, b_ref, o_ref):
    a = _silu(c_ref[...]).astype(BF16)
    o_ref[...] = jnp.dot(a, w_ref[...].astype(BF16), preferred_element_type=F32) + b_ref[...]


def _modulation(cc, w_ada, b_ada):
    n, d = cc.shape
    cols = w_ada.shape[1]
    return pl.pallas_call(
        _mod_kernel,
        grid=(cols // COL_TILE,),
        in_specs=[
            pl.BlockSpec((n, d), lambda j: (0, 0)),
            pl.BlockSpec((d, COL_TILE), lambda j: (0, j)),
            pl.BlockSpec((1, COL_TILE), lambda j: (0, j)),
        ],
        out_specs=pl.BlockSpec((n, COL_TILE), lambda j: (0, j)),
        out_shape=jax.ShapeDtypeStruct((n, cols), F32),
        compiler_params=pltpu.CompilerParams(
            dimension_semantics=("parallel",), vmem_limit_bytes=VMEM_LIMIT),
        name="adaln_mod",
    )(cc, w_ada, b_ada.reshape(1, cols))


def _rope_heads(ref, t, cos, sin, scale):
    for hh in range(N_HEADS):
        th = _head_cols(t, hh)
        rot = pltpu.roll(th, HEAD_W // 2, 1)
        ref[0, 0, hh] = ((th * cos + rot * sin) * scale).astype(ref.dtype)


def _store_heads_transposed(ref, t):
    for hh in range(N_HEADS):
        ref[0, hh] = _head_cols(t, hh).T.astype(ref.dtype)


def _norm_to_scratch(x_ref, mod_ref, h_ref):
    shift = mod_ref[0, 0:1, :]
    scale = mod_ref[0, 1:2, :]
    h_ref[...] = _rms_mod(x_ref[0], shift, scale).astype(BF16)


def _inproj_kernel(x_ref, mod_ref, w_ref, cos_ref, sin_ref,
                   ug_ref, gz_ref, qk_ref, vt_ref, h_ref, t_ref, *, n_pair, n_act):
    j = pl.program_id(2)
    half = COL_TILE // 2
    n_pe = n_pair + n_act

    def project():
        t_ref[...] = jnp.dot(h_ref[...], w_ref[...], preferred_element_type=F32)

    def previous():
        return t_ref[...]

    @pl.when(j == 0)
    def _():
        _norm_to_scratch(x_ref, mod_ref, h_ref)
        project()

    @pl.when((j >= 1) & (j <= n_pair // 2))
    def _():
        t = previous()
        ug_ref[0] = (t[:, half:] * t[:, :half]).astype(ug_ref.dtype)
        project()

    @pl.when((j > n_pair // 2) & (j <= n_pair))
    def _():
        t = previous()
        ug_ref[0] = (t[:, :half] * _silu(t[:, half:])).astype(ug_ref.dtype)
        project()

    @pl.when((j > n_pair) & (j < n_pe))
    def _():
        gz_ref[0] = jax.nn.sigmoid(previous()).astype(gz_ref.dtype)
        project()

    @pl.when(j == n_pe)
    def _():
        gz_ref[0] = _silu(previous()).astype(gz_ref.dtype)
        project()

    @pl.when(j == n_pe + 1)
    def _():
        _rope_heads(qk_ref, previous(), cos_ref[...], sin_ref[...], QK_SCALE * LOG2E)
        project()

    @pl.when(j == n_pe + 2)
    def _():
        _rope_heads(qk_ref, previous(), cos_ref[...], sin_ref[...], 1.0)
        _store_heads_transposed(
            vt_ref, jnp.dot(h_ref[...], w_ref[...], preferred_element_type=F32))


def _inproj(x, mod, w, cos_t, sin_t, n_pair, n_act):
    b, s, d = x.shape
    tm = min(ROW_TILE, s)
    n_pe = n_pair + n_act
    half = COL_TILE // 2
    return pl.pallas_call(
        functools.partial(_inproj_kernel, n_pair=n_pair, n_act=n_act),
        grid=(b, s // tm, n_pe + 3),
        in_specs=[
            pl.BlockSpec((1, tm, d), lambda bi, i, j: (bi, i, 0)),
            pl.BlockSpec((1, 3, d), lambda bi, i, j: (bi, 0, 0)),
            pl.BlockSpec((d, COL_TILE), lambda bi, i, j: (0, j)),
            pl.BlockSpec((tm, HEAD_W), lambda bi, i, j: (i, 0)),
            pl.BlockSpec((tm, HEAD_W), lambda bi, i, j: (i, 0)),
        ],
        out_specs=[
            pl.BlockSpec((1, tm, half), lambda bi, i, j: (bi, i, jnp.clip(j - 1, 0, n_pair - 1))),
            pl.BlockSpec((1, tm, COL_TILE),
                         lambda bi, i, j: (bi, i, jnp.clip(j - 1 - n_pair, 0, n_act - 1))),
            pl.BlockSpec((1, 1, N_HEADS, tm, HEAD_W),
                         lambda bi, i, j: (bi, jnp.clip(j - 1 - n_pe, 0, 1), 0, i, 0)),
            pl.BlockSpec((1, N_HEADS, HEAD_W, tm), lambda bi, i, j: (bi, 0, 0, i)),
        ],
        out_shape=[
            jax.ShapeDtypeStruct((b, s, n_pair * half), BF16),
            jax.ShapeDtypeStruct((b, s, n_act * COL_TILE), BF16),
            jax.ShapeDtypeStruct((b, 2, N_HEADS, s, HEAD_W), BF16),
            jax.ShapeDtypeStruct((b, N_HEADS, HEAD_W, s), BF16),
        ],
        scratch_shapes=[pltpu.VMEM((tm, d), BF16), pltpu.VMEM((tm, COL_TILE), F32)],
        compiler_params=pltpu.CompilerParams(
            dimension_semantics=("parallel", "parallel", "arbitrary"),
            vmem_limit_bytes=VMEM_LIMIT),
        name="in_proj",
    )(x, mod, w, cos_t, sin_t)


def _ctx_kernel(x_ref, mod_ref, w_ref, k_ref, vt_ref, h_ref):
    j = pl.program_id(0)
    bi = pl.program_id(1)

    @pl.when(j == 0)
    def _():
        _norm_to_scratch(x_ref, mod_ref, h_ref.at[bi])
        t = jnp.dot(h_ref[bi], w_ref[...], preferred_element_type=F32)
        for hh in range(N_HEADS):
            k_ref[0, hh] = _head_cols(t, hh).astype(k_ref.dtype)

    @pl.when(j == 1)
    def _():
        _store_heads_transposed(
            vt_ref, jnp.dot(h_ref[bi], w_ref[...], preferred_element_type=F32))


def _ctx_proj(ctx, mod, w, k_col):
    b, c, d = ctx.shape
    ctx_row = mod.shape[0] - 1
    return pl.pallas_call(
        _ctx_kernel,
        grid=(2, b),
        in_specs=[
            pl.BlockSpec((1, c, d), lambda j, bi: (jnp.where(j == 0, bi, b - 1), 0, 0)),
            pl.BlockSpec((1, 3, d), lambda j, bi: (ctx_row, 0, 0)),
            pl.BlockSpec((d, COL_TILE), lambda j, bi: (0, k_col + j)),
        ],
        out_specs=[
            pl.BlockSpec((1, N_HEADS, c, HEAD_W),
                         lambda j, bi: (jnp.where(j == 0, bi, b - 1), 0, 0, 0)),
            pl.BlockSpec((1, N_HEADS, HEAD_W, c),
                         lambda j, bi: (jnp.where(j == 0, 0, bi), 0, 0, 0)),
        ],
        out_shape=[
            jax.ShapeDtypeStruct((b, N_HEADS, c, HEAD_W), BF16),
            jax.ShapeDtypeStruct((b, N_HEADS, HEAD_W, c), BF16),
        ],
        scratch_shapes=[pltpu.VMEM((b, c, d), BF16)],
        compiler_params=pltpu.CompilerParams(
            dimension_semantics=("arbitrary", "arbitrary"), vmem_limit_bytes=VMEM_LIMIT),
        name="ctx_proj",
    )(ctx, mod, w)


def _nt_dot(a, b):
    return lax.dot_general(a, b, (((1,), (1,)), ((), ())), preferred_element_type=F32)


def _fold_rows(x, op):
    return op(x.reshape(x.shape[0] // SUBLANES, SUBLANES, x.shape[1]), axis=0)


def _attn_kernel(dl_ref, sub_ref, q_ref, kl_ref, kc_ref, vtl_ref, vtc_ref, o_ref,
                 s_ref, sc_ref, p_ref, pc_ref, ma_ref, mb_ref, fa_ref, fb_ref, ra_ref, rb_ref,
                 *, tq, ck, lam_init):
    n_heads, s_len, _ = q_ref.shape
    n_tiles = s_len // tq
    tile_bits = n_tiles.bit_length() - 1
    n_steps = n_heads * n_tiles
    m_bufs = (ma_ref, mb_ref)
    f_bufs = ((fa_ref, ra_ref), (fb_ref, rb_ref))

    dl = dl_ref[...]
    lam = (jnp.exp(jnp.sum(dl[0:1] * dl[1:2], axis=1, keepdims=True))
           - jnp.exp(jnp.sum(dl[2:3] * dl[3:4], axis=1, keepdims=True)) + lam_init)
    sub = jnp.broadcast_to(sub_ref[...] * (1.0 - lam_init), (HEAD_W, tq))
    lane = lax.broadcasted_iota(jnp.int32, (1, HEAD_W), 1)
    comp0 = (lane % HEAD_DIM) < (HEAD_DIM // 2)

    def head_tile(t):
        if isinstance(t, int):
            return divmod(t, n_tiles)
        return lax.shift_right_logical(t, tile_bits), lax.bitwise_and(t, n_tiles - 1)

    def step(t_out, t_exp, t_qk, par):
        f_ref, r_ref = f_bufs[par]
        m_exp, m_qk = m_bufs[1 - par], m_bufs[par]
        if t_out is not None:
            h_out, i_out = head_tile(t_out)
        if t_qk is not None:
            h_qk, i_qk = head_tile(t_qk)
            q = q_ref[h_qk, pl.ds(pl.multiple_of(i_qk * tq, tq), tq), :]
            zero = jnp.zeros_like(q)
            lhs = jnp.concatenate([jnp.where(comp0, q, zero), jnp.where(comp0, zero, q)], axis=0)
        acc = m8 = l8 = None
        for c in range(-1, s_len // ck):
            rows = slice(None) if c < 0 else slice(c * ck, (c + 1) * ck)
            s_c, p_c = (sc_ref, pc_ref) if c < 0 else (s_ref.at[c], p_ref.at[c])
            if t_out is not None:
                vt = vtc_ref[h_out] if c < 0 else vtl_ref[h_out, :, rows]
                p = p_c[...]
                n = p.shape[0]
                p0 = p[:, 0:tq].reshape(n // BF16_ROWS, BF16_ROWS, tq)
                p1 = p[:, tq:2 * tq].reshape(n // BF16_ROWS, BF16_ROWS, tq)
                a = (p0 - f_ref[...] * p1).reshape(n, tq)
                part = jnp.dot(vt, a, preferred_element_type=F32)
                acc = part if acc is None else acc + part
            if t_exp is not None:
                s = s_c[...]
                n = s.shape[0]
                p = jnp.exp2(s.reshape(n // SUBLANES, SUBLANES, 2 * tq) - m_exp[...])
                p_c[...] = p.reshape(n, 2 * tq).astype(BF16)
                part = jnp.sum(p, axis=0)
                l8 = part if l8 is None else l8 + part
            if t_qk is not None:
                keys = kc_ref[h_qk] if c < 0 else kl_ref[h_qk, rows, :]
                s = _nt_dot(keys, lhs)
                s_c[...] = s
                part = _fold_rows(s, jnp.max)
                m8 = part if m8 is None else jnp.maximum(m8, part)
        if t_out is not None:
            o = (acc.reshape(HEAD_W // SUBLANES, SUBLANES, tq) * r_ref[...]).reshape(HEAD_W, tq)
            o = o * lax.rsqrt(jnp.mean(o * o, axis=0, keepdims=True) + RMS_EPS) * sub
            o_ref[0, h_out, pl.ds(pl.multiple_of(i_out * tq, tq), tq), :] = o.T.astype(o_ref.dtype)
        if t_exp is not None:
            l = jnp.sum(l8, axis=0, keepdims=True)
            l0, l1 = l[:, 0:tq], l[:, tq:2 * tq]
            f_next, r_next = f_bufs[1 - par]
            f_next[...] = jnp.broadcast_to(lam * l0 / l1, f_next.shape).astype(BF16)
            r_next[...] = jnp.broadcast_to(1.0 / l0, r_next.shape)
        if t_qk is not None:
            m_qk[...] = jnp.broadcast_to(jnp.max(m8, axis=0, keepdims=True), m_qk.shape)

    step(None, None, 0, 0)
    step(None, 0, 1, 1)

    def pair(u, carry):
        t = 2 * u
        step(t, t + 1, t + 2, 0)
        step(t + 1, t + 2, t + 3, 1)
        return carry

    lax.fori_loop(0, n_steps // 2 - 1, pair, 0)
    step(n_steps - 2, n_steps - 1, None, 0)
    step(n_steps - 1, None, None, 1)


def _attention(dl, subln, qk, vt, k_ctx, vt_ctx, lam_init):
    b, _, h, s, _ = qk.shape
    c = k_ctx.shape[2]
    hg = min(HEAD_GROUP, h)
    tq = min(Q_TILE, s // 2)
    ck = min(KEY_CHUNK, s)
    n_tiles = s // tq
    assert s % (2 * tq) == 0 and s % ck == 0 and h % hg == 0 and n_tiles & (n_tiles - 1) == 0
    m_buf = pltpu.VMEM((SUBLANES, 2 * tq), F32)
    f_buf = pltpu.VMEM((BF16_ROWS, tq), BF16)
    r_buf = pltpu.VMEM((SUBLANES, tq), F32)
    return pl.pallas_call(
        functools.partial(_attn_kernel, tq=tq, ck=ck, lam_init=lam_init),
        grid=(b, h // hg),
        in_specs=[
            pl.BlockSpec(dl.shape, lambda bi, gi: (0, 0)),
            pl.BlockSpec(subln.shape, lambda bi, gi: (0, 0)),
            pl.BlockSpec((None, None, hg, s, HEAD_W), lambda bi, gi: (bi, 0, gi, 0, 0)),
            pl.BlockSpec((None, None, hg, s, HEAD_W), lambda bi, gi: (bi, 1, gi, 0, 0)),
            pl.BlockSpec((None, hg, c, HEAD_W), lambda bi, gi: (bi, gi, 0, 0)),
            pl.BlockSpec((None, hg, HEAD_W, s), lambda bi, gi: (bi, gi, 0, 0)),
            pl.BlockSpec((None, hg, HEAD_W, c), lambda bi, gi: (bi, gi, 0, 0)),
        ],
        out_specs=pl.BlockSpec((1, hg, s, HEAD_W), lambda bi, gi: (bi, gi, 0, 0)),
        out_shape=jax.ShapeDtypeStruct((b, h, s, HEAD_W), F32),
        scratch_shapes=[
            pltpu.VMEM((s // ck, ck, 2 * tq), F32), pltpu.VMEM((c, 2 * tq), F32),
            pltpu.VMEM((s // ck, ck, 2 * tq), BF16), pltpu.VMEM((c, 2 * tq), BF16),
            m_buf, m_buf, f_buf, f_buf, r_buf, r_buf,
        ],
        compiler_params=pltpu.CompilerParams(
            dimension_semantics=("parallel", "parallel"), vmem_limit_bytes=VMEM_LIMIT),
        name="diff_attn",
    )(dl, subln, qk, qk, k_ctx, vt, vt_ctx)


def _out_kernel(u_ref, g2_ref, up_ref, un_ref, sig_ref, szb_ref, o_ref, x_ref, mod_ref, cw_ref,
                woa_ref, wob_ref, wo_ref, fw_ref, out_ref, *, tm):
    i = pl.program_id(1)
    last = pl.num_programs(1) - 1
    d = x_ref.shape[2]

    u = u_ref[0].astype(F32)
    u_prev = jnp.where(i > 0, up_ref[0, BF16_ROWS - 1:BF16_ROWS, :].astype(F32), 0.0)
    u_next = jnp.where(i < last, un_ref[0, 0:1, :].astype(F32), 0.0)
    row = lax.broadcasted_iota(jnp.int32, (tm, 1), 0)
    u_dn = jnp.where(row == 0, u_prev, pltpu.roll(u, 1, 0))
    u_up = jnp.where(row == tm - 1, u_next, pltpu.roll(u, tm - 1, 0))
    y = cw_ref[0:1, :] * u_dn + cw_ref[1:2, :] * u + cw_ref[2:3, :] * u_up
    a_in = (g2_ref[0].astype(F32) * y).astype(BF16)
    y_a = jnp.dot(a_in, woa_ref[...], preferred_element_type=F32)

    o = jnp.concatenate([o_ref[0, hh] for hh in range(N_HEADS)], axis=1)
    b_in = (o * szb_ref[0].astype(F32)).astype(BF16)
    y_b = jnp.dot(b_in, wob_ref[...], preferred_element_type=F32)

    merged = (sig_ref[0, :, 0:d].astype(F32) * y_a
              + sig_ref[0, :, d:2 * d].astype(F32) * y_b).astype(BF16)
    y_o = jnp.dot(merged, wo_ref[...], preferred_element_type=F32)

    xo = x_ref[0] + mod_ref[0, 2:3, :] * y_o
    out_ref[0] = xo * lax.rsqrt(jnp.mean(xo * xo, axis=-1, keepdims=True) + RMS_EPS) * fw_ref[...]


def _out_stage(ug, gz, o, x, mod, conv_w, woa, wob, wo, fw):
    b, s, d = x.shape
    cw = conv_w.shape[-1]
    tm = min(OUT_TILE, s)
    tb = tm // BF16_ROWS
    nblk = s // BF16_ROWS
    col = lambda cb: (lambda bi, i: (bi, i, cb))
    const2 = lambda bi, i: (0, 0)
    resident = dict(pipeline_mode=pl.Buffered(1))
    return pl.pallas_call(
        functools.partial(_out_kernel, tm=tm),
        grid=(b, s // tm),
        in_specs=[
            pl.BlockSpec((1, tm, cw), col(0)),
            pl.BlockSpec((1, tm, cw), col(1)),
            pl.BlockSpec((1, BF16_ROWS, cw), lambda bi, i: (bi, jnp.maximum(i * tb - 1, 0), 0)),
            pl.BlockSpec((1, BF16_ROWS, cw),
                         lambda bi, i: (bi, jnp.minimum((i + 1) * tb, nblk - 1), 0)),
            pl.BlockSpec((1, tm, 2 * d), col(0)),
            pl.BlockSpec((1, tm, COL_TILE), col(2 * d // COL_TILE)),
            pl.BlockSpec((1, N_HEADS, tm, HEAD_W), lambda bi, i: (bi, 0, i, 0)),
            pl.BlockSpec((1, tm, d), col(0)),
            pl.BlockSpec((1, 3, d), lambda bi, i: (bi, 0, 0)),
            pl.BlockSpec(conv_w.shape, const2),
            pl.BlockSpec(woa.shape, const2, **resident),
            pl.BlockSpec(wob.shape, const2, **resident),
            pl.BlockSpec(wo.shape, const2, **resident),
            pl.BlockSpec(fw.shape, const2),
        ],
        out_specs=pl.BlockSpec((1, tm, d), col(0)),
        out_shape=jax.ShapeDtypeStruct((b, s, d), x.dtype),
        compiler_params=pltpu.CompilerParams(
            dimension_semantics=("parallel", "parallel"), vmem_limit_bytes=VMEM_LIMIT),
        name="out_stage",
    )(ug, ug, ug, ug, gz, gz, o, x, mod, conv_w, woa, wob, wo, fw)


def _head_lane_order(w):
    lead = w.shape[:-1]
    w = w.reshape(lead + (N_HEADS, 2, 2, 2, AXIS_ROT // 2))
    n = len(lead)
    w = jnp.transpose(w, tuple(range(n)) + (n, n + 3, n + 1, n + 2, n + 4))
    return w.reshape(lead + (N_HEADS * HEAD_W,))


def _rope_tables(n_tokens):
    rows = n_tokens // GRID_W
    row = jnp.repeat(jnp.arange(rows, dtype=F32), GRID_W)
    col = jnp.tile(jnp.arange(GRID_W, dtype=F32), rows)
    inv_freq = ROPE_THETA ** (-jnp.arange(0, AXIS_ROT, 2, dtype=F32) / AXIS_ROT)
    ang = jnp.concatenate([row[:, None] * inv_freq, col[:, None] * inv_freq], axis=-1)
    ang = jnp.tile(ang, (1, 4))
    sign = jnp.where(jnp.arange(HEAD_W) < HEAD_W // 2, -1.0, 1.0).astype(F32)
    return jnp.cos(ang), jnp.sin(ang) * sign


def kernel(x, c, ctx, c_ctx, w_ada, b_ada, w_in, conv_w, diff_lambda, subln_w,
           w_out_a, w_out_b, w_o, final_norm_w):
    b, s, d = x.shape
    cw = conv_w.shape[-1]
    aw = w_out_b.shape[1]
    assert w_in.shape[0] == 1, "single-layer block"
    assert cw == COL_TILE and aw == COL_TILE == N_HEADS * HEAD_W and d == 2 * COL_TILE
    assert s % GRID_W == 0 and s % min(ROW_TILE, s) == 0 and ctx.shape[1] % LANES == 0
    lam_init = 0.8 - 0.6 * math.exp(-0.3 * 0)

    wi = w_in[0]
    half = COL_TILE // 2
    a_end = 4 * cw
    wa = wi[:, :a_end].reshape(d, 2, 2, cw // half, half)
    wa = jnp.transpose(wa, (0, 2, 3, 1, 4)).reshape(d, a_end)
    wq, wk, wv, wzb = (wi[:, a_end + n * aw:a_end + (n + 1) * aw] for n in range(4))
    wg = wi[:, a_end + 4 * aw:]
    w = jnp.concatenate(
        [wa, wg, wzb, _head_lane_order(wq), _head_lane_order(wk), wv], axis=1).astype(BF16)
    n_pair = 4 * cw // COL_TILE
    n_act = (2 * d + aw) // COL_TILE

    cc = jnp.concatenate([c, c_ctx[None, :]], axis=0)
    mod = _modulation(cc, w_ada[0], b_ada[0]).reshape(b + 1, 3, d)

    cos_t, sin_t = _rope_tables(s)
    ug, gz, qk, vt = _inproj(x, mod, w, cos_t, sin_t, n_pair, n_act)
    k_ctx, vt_ctx = _ctx_proj(ctx, mod, w, n_pair + n_act + 1)

    o = _attention(diff_lambda[0], subln_w[0].reshape(HEAD_W, 1), qk, vt, k_ctx, vt_ctx, lam_init)

    return _out_stage(ug, gz, o, x, mod, conv_w[0],
                      w_out_a[0].astype(BF16), w_out_b[0].astype(BF16), w_o[0].astype(BF16),
                      final_norm_w.reshape(1, d))
```

```python
import functools
import math

import jax
import jax.numpy as jnp
from jax import lax
from jax.experimental import pallas as pl
from jax.experimental.pallas import tpu as pltpu

F32 = jnp.float32
BF16 = jnp.bfloat16

N_HEADS = 8
HEAD_DIM = 64
HEAD_W = 2 * HEAD_DIM
GRID_W = 64
ROPE_THETA = 10000.0
AXIS_ROT = HEAD_DIM // 2
RMS_EPS = 1e-6
QK_SCALE = HEAD_DIM ** -0.5
LOG2E = math.log2(math.e)
LANES = 128
SUBLANES = 8
BF16_ROWS = 16
COL_TILE = 1024

ROW_TILE = 1024
Q_TILE = 256
KEY_CHUNK = 512
HEAD_GROUP = 2
OUT_TILE = 512
VMEM_LIMIT = 56 * 1024 * 1024


def _silu(x):
    return x * jax.nn.sigmoid(x)


def _rms_mod(x, shift, scale):
    ms = jnp.mean(x * x, axis=-1, keepdims=True)
    return (x * lax.rsqrt(ms + RMS_EPS)) * (1.0 + scale) + shift


def _head_cols(t, hh):
    return t[:, hh * HEAD_W:(hh + 1) * HEAD_W]


def _mod_kernel(c_ref, w_ref, b_ref, o_ref):
    a = _silu(c_ref[...]).astype(BF16)
    o_ref[...] = jnp.dot(a, w_ref[...].astype(BF16), preferred_element_type=F32) + b_ref[...]


def _modulation(cc, w_ada, b_ada):
    n, d = cc.shape
    cols = w_ada.shape[1]
    return pl.pallas_call(
        _mod_kernel,
        grid=(cols // COL_TILE,),
        in_specs=[
            pl.BlockSpec((n, d), lambda j: (0, 0)),
            pl.BlockSpec((d, COL_TILE), lambda j: (0, j)),
            pl.BlockSpec((1, COL_TILE), lambda j: (0, j)),
        ],
        out_specs=pl.BlockSpec((n, COL_TILE), lambda j: (0, j)),
        out_shape=jax.ShapeDtypeStruct((n, cols), F32),
        compiler_params=pltpu.CompilerParams(
            dimension_semantics=("parallel",), vmem_limit_bytes=VMEM_LIMIT),
        name="adaln_mod",
    )(cc, w_ada, b_ada.reshape(1, cols))


def _rope_heads(ref, t, cos, sin, scale):
    for hh in range(N_HEADS):
        th = _head_cols(t, hh)
        rot = pltpu.roll(th, HEAD_W // 2, 1)
        ref[0, 0, hh] = ((th * cos + rot * sin) * scale).astype(ref.dtype)


def _store_heads_transposed(ref, t):
    for hh in range(N_HEADS):
        ref[0, hh] = _head_cols(t, hh).T.astype(ref.dtype)


def _norm_to_scratch(x_ref, mod_ref, h_ref):
    shift = mod_ref[0, 0:1, :]
    scale = mod_ref[0, 1:2, :]
    h_ref[...] = _rms_mod(x_ref[0], shift, scale).astype(BF16)


def _inproj_kernel(x_ref, mod_ref, w_ref, cos_ref, sin_ref,
                   ug_ref, gz_ref, qk_ref, vt_ref, h_ref, t_ref, *, n_gate):
    j = pl.program_id(2)
    first_gate = 4
    zb_step = first_gate + n_gate

    def project():
        return jnp.dot(h_ref[...], w_ref[...], preferred_element_type=F32)

    @pl.when(j == 0)
    def _():
        _norm_to_scratch(x_ref, mod_ref, h_ref)
        t_ref[...] = project()

    @pl.when(j == 1)
    def _():
        ug_ref[0] = (t_ref[...] * project()).astype(ug_ref.dtype)

    @pl.when((j == 2) | (j == first_gate))
    def _():
        t_ref[...] = project()

    @pl.when(j == 3)
    def _():
        ug_ref[0] = (project() * _silu(t_ref[...])).astype(ug_ref.dtype)

    @pl.when((j > first_gate) & (j <= zb_step))
    def _():
        gz_ref[0] = jax.nn.sigmoid(t_ref[...]).astype(gz_ref.dtype)
        t_ref[...] = project()

    @pl.when(j == zb_step + 1)
    def _():
        gz_ref[0] = _silu(t_ref[...]).astype(gz_ref.dtype)
        t_ref[...] = project()

    @pl.when(j == zb_step + 2)
    def _():
        _rope_heads(qk_ref, t_ref[...], cos_ref[...], sin_ref[...], QK_SCALE * LOG2E)
        t_ref[...] = project()

    @pl.when(j == zb_step + 3)
    def _():
        _rope_heads(qk_ref, t_ref[...], cos_ref[...], sin_ref[...], 1.0)
        _store_heads_transposed(vt_ref, project())


def _inproj_steps(n_gate):
    xa, gb, gc, za, q, k, v, zb = range(8)
    return [gc, xa, za, gb] + [8 + n for n in range(n_gate)] + [zb, q, k, v]


def _inproj(x, mod, w, cos_t, sin_t):
    b, s, d = x.shape
    tm = min(ROW_TILE, s)
    n_gate = 2 * d // COL_TILE
    steps = _inproj_steps(n_gate)
    n_steps = len(steps)

    def w_tile(bi, i, j):
        tile = steps[0]
        for n in range(1, n_steps):
            tile = jnp.where(j == n, steps[n], tile)
        return 0, tile

    return pl.pallas_call(
        functools.partial(_inproj_kernel, n_gate=n_gate),
        grid=(b, s // tm, n_steps),
        in_specs=[
            pl.BlockSpec((1, tm, d), lambda bi, i, j: (bi, i, 0)),
            pl.BlockSpec((1, 3, d), lambda bi, i, j: (bi, 0, 0)),
            pl.BlockSpec((d, COL_TILE), w_tile),
            pl.BlockSpec((tm, HEAD_W), lambda bi, i, j: (i, 0)),
            pl.BlockSpec((tm, HEAD_W), lambda bi, i, j: (i, 0)),
        ],
        out_specs=[
            pl.BlockSpec((1, tm, COL_TILE), lambda bi, i, j: (bi, i, jnp.where(j <= 1, 0, 1))),
            pl.BlockSpec((1, tm, COL_TILE), lambda bi, i, j: (bi, i, jnp.clip(j - 5, 0, n_gate))),
            pl.BlockSpec((1, 1, N_HEADS, tm, HEAD_W),
                         lambda bi, i, j: (bi, jnp.clip(j - (n_steps - 2), 0, 1), 0, i, 0)),
            pl.BlockSpec((1, N_HEADS, HEAD_W, tm), lambda bi, i, j: (bi, 0, 0, i)),
        ],
        out_shape=[
            jax.ShapeDtypeStruct((b, s, 2 * COL_TILE), BF16),
            jax.ShapeDtypeStruct((b, s, (n_gate + 1) * COL_TILE), BF16),
            jax.ShapeDtypeStruct((b, 2, N_HEADS, s, HEAD_W), BF16),
            jax.ShapeDtypeStruct((b, N_HEADS, HEAD_W, s), BF16),
        ],
        scratch_shapes=[pltpu.VMEM((tm, d), BF16), pltpu.VMEM((tm, COL_TILE), F32)],
        compiler_params=pltpu.CompilerParams(
            dimension_semantics=("parallel", "parallel", "arbitrary"),
            vmem_limit_bytes=VMEM_LIMIT),
        name="in_proj",
    )(x, mod, w, cos_t, sin_t)


def _ctx_kernel(x_ref, mod_ref, w_ref, k_ref, vt_ref, h_ref):
    j = pl.program_id(0)
    bi = pl.program_id(1)

    @pl.when(j == 0)
    def _():
        _norm_to_scratch(x_ref, mod_ref, h_ref.at[bi])
        t = jnp.dot(h_ref[bi], w_ref[...], preferred_element_type=F32)
        for hh in range(N_HEADS):
            k_ref[0, hh] = _head_cols(t, hh).astype(k_ref.dtype)

    @pl.when(j == 1)
    def _():
        _store_heads_transposed(
            vt_ref, jnp.dot(h_ref[bi], w_ref[...], preferred_element_type=F32))


def _ctx_proj(ctx, mod, w, k_col):
    b, c, d = ctx.shape
    ctx_row = mod.shape[0] - 1
    return pl.pallas_call(
        _ctx_kernel,
        grid=(2, b),
        in_specs=[
            pl.BlockSpec((1, c, d), lambda j, bi: (jnp.where(j == 0, bi, b - 1), 0, 0)),
            pl.BlockSpec((1, 3, d), lambda j, bi: (ctx_row, 0, 0)),
            pl.BlockSpec((d, COL_TILE), lambda j, bi: (0, k_col + j)),
        ],
        out_specs=[
            pl.BlockSpec((1, N_HEADS, c, HEAD_W),
                         lambda j, bi: (jnp.where(j == 0, bi, b - 1), 0, 0, 0)),
            pl.BlockSpec((1, N_HEADS, HEAD_W, c),
                         lambda j, bi: (jnp.where(j == 0, 0, bi), 0, 0, 0)),
        ],
        out_shape=[
            jax.ShapeDtypeStruct((b, N_HEADS, c, HEAD_W), BF16),
            jax.ShapeDtypeStruct((b, N_HEADS, HEAD_W, c), BF16),
        ],
        scratch_shapes=[pltpu.VMEM((b, c, d), BF16)],
        compiler_params=pltpu.CompilerParams(
            dimension_semantics=("arbitrary", "arbitrary"), vmem_limit_bytes=VMEM_LIMIT),
        name="ctx_proj",
    )(ctx, mod, w)


def _nt_dot(a, b):
    return lax.dot_general(a, b, (((1,), (1,)), ((), ())), preferred_element_type=F32)


def _fold_rows(x, op):
    return op(x.reshape(x.shape[0] // SUBLANES, SUBLANES, x.shape[1]), axis=0)


def _attn_kernel(dl_ref, sub_ref, q_ref, kl_ref, kc_ref, vtl_ref, vtc_ref, o_ref,
                 s_ref, sc_ref, p_ref, pc_ref, ma_ref, mb_ref, fa_ref, fb_ref, ra_ref, rb_ref,
                 *, tq, ck, lam_init):
    n_heads, s_len, _ = q_ref.shape
    n_tiles = s_len // tq
    tile_bits = n_tiles.bit_length() - 1
    n_steps = n_heads * n_tiles
    m_bufs = (ma_ref, mb_ref)
    f_bufs = ((fa_ref, ra_ref), (fb_ref, rb_ref))

    dl = dl_ref[...]
    lam = (jnp.exp(jnp.sum(dl[0:1] * dl[1:2], axis=1, keepdims=True))
           - jnp.exp(jnp.sum(dl[2:3] * dl[3:4], axis=1, keepdims=True)) + lam_init)
    sub = jnp.broadcast_to(sub_ref[...] * (1.0 - lam_init), (HEAD_W, tq))
    lane = lax.broadcasted_iota(jnp.int32, (1, HEAD_W), 1)
    comp0 = (lane % HEAD_DIM) < (HEAD_DIM // 2)

    def head_tile(t):
        if isinstance(t, int):
            return divmod(t, n_tiles)
        return lax.shift_right_logical(t, tile_bits), lax.bitwise_and(t, n_tiles - 1)

    def step(t_out, t_exp, t_qk, par):
        f_ref, r_ref = f_bufs[par]
        m_exp, m_qk = m_bufs[1 - par], m_bufs[par]
        if t_out is not None:
            h_out, i_out = head_tile(t_out)
        if t_qk is not None:
            h_qk, i_qk = head_tile(t_qk)
            q = q_ref[h_qk, pl.ds(pl.multiple_of(i_qk * tq, tq), tq), :]
            zero = jnp.zeros_like(q)
            lhs = jnp.concatenate([jnp.where(comp0, q, zero), jnp.where(comp0, zero, q)], axis=0)
        acc = m8 = l8 = None
        for c in range(-1, s_len // ck):
            rows = slice(None) if c < 0 else slice(c * ck, (c + 1) * ck)
            s_c, p_c = (sc_ref, pc_ref) if c < 0 else (s_ref.at[c], p_ref.at[c])
            if t_out is not None:
                vt = vtc_ref[h_out] if c < 0 else vtl_ref[h_out, :, rows]
                p = p_c[...]
                n = p.shape[0]
                p0 = p[:, 0:tq].reshape(n // BF16_ROWS, BF16_ROWS, tq)
                p1 = p[:, tq:2 * tq].reshape(n // BF16_ROWS, BF16_ROWS, tq)
                a = (p0 - f_ref[...] * p1).reshape(n, tq)
                part = jnp.dot(vt, a, preferred_element_type=F32)
                acc = part if acc is None else acc + part
            if t_exp is not None:
                s = s_c[...]
                n = s.shape[0]
                p = jnp.exp2(s.reshape(n // SUBLANES, SUBLANES, 2 * tq) - m_exp[...])
                p_c[...] = p.reshape(n, 2 * tq).astype(BF16)
                part = jnp.sum(p, axis=0)
                l8 = part if l8 is None else l8 + part
            if t_qk is not None:
                keys = kc_ref[h_qk] if c < 0 else kl_ref[h_qk, rows, :]
                s = _nt_dot(keys, lhs)
                s_c[...] = s
                part = _fold_rows(s, jnp.max)
                m8 = part if m8 is None else jnp.maximum(m8, part)
        if t_out is not None:
            o = (acc.reshape(HEAD_W // SUBLANES, SUBLANES, tq) * r_ref[...]).reshape(HEAD_W, tq)
            o = o * lax.rsqrt(jnp.mean(o * o, axis=0, keepdims=True) + RMS_EPS) * sub
            o_ref[0, h_out, pl.ds(pl.multiple_of(i_out * tq, tq), tq), :] = o.T.astype(o_ref.dtype)
        if t_exp is not None:
            l = jnp.sum(l8, axis=0, keepdims=True)
            l0, l1 = l[:, 0:tq], l[:, tq:2 * tq]
            f_next, r_next = f_bufs[1 - par]
            f_next[...] = jnp.broadcast_to(lam * l0 / l1, f_next.shape).astype(BF16)
            r_next[...] = jnp.broadcast_to(1.0 / l0, r_next.shape)
        if t_qk is not None:
            m_qk[...] = jnp.broadcast_to(jnp.max(m8, axis=0, keepdims=True), m_qk.shape)

    step(None, None, 0, 0)
    step(None, 0, 1, 1)

    def pair(u, carry):
        t = 2 * u
        step(t, t + 1, t + 2, 0)
        step(t + 1, t + 2, t + 3, 1)
        return carry

    lax.fori_loop(0, n_steps // 2 - 1, pair, 0)
    step(n_steps - 2, n_steps - 1, None, 0)
    step(n_steps - 1, None, None, 1)


def _attention(dl, subln, qk, vt, k_ctx, vt_ctx, lam_init):
    b, _, h, s, _ = qk.shape
    c = k_ctx.shape[2]
    hg = min(HEAD_GROUP, h)
    tq = min(Q_TILE, s // 2)
    ck = min(KEY_CHUNK, s)
    n_tiles = s // tq
    assert s % (2 * tq) == 0 and s % ck == 0 and h % hg == 0 and n_tiles & (n_tiles - 1) == 0
    m_buf = pltpu.VMEM((SUBLANES, 2 * tq), F32)
    f_buf = pltpu.VMEM((BF16_ROWS, tq), BF16)
    r_buf = pltpu.VMEM((SUBLANES, tq), F32)
    return pl.pallas_call(
        functools.partial(_attn_kernel, tq=tq, ck=ck, lam_init=lam_init),
        grid=(b, h // hg),
        in_specs=[
            pl.BlockSpec(dl.shape, lambda bi, gi: (0, 0)),
            pl.BlockSpec(subln.shape, lambda bi, gi: (0, 0)),
            pl.BlockSpec((None, None, hg, s, HEAD_W), lambda bi, gi: (bi, 0, gi, 0, 0)),
            pl.BlockSpec((None, None, hg, s, HEAD_W), lambda bi, gi: (bi, 1, gi, 0, 0)),
            pl.BlockSpec((None, hg, c, HEAD_W), lambda bi, gi: (bi, gi, 0, 0)),
            pl.BlockSpec((None, hg, HEAD_W, s), lambda bi, gi: (bi, gi, 0, 0)),
            pl.BlockSpec((None, hg, HEAD_W, c), lambda bi, gi: (bi, gi, 0, 0)),
        ],
        out_specs=pl.BlockSpec((1, hg, s, HEAD_W), lambda bi, gi: (bi, gi, 0, 0)),
        out_shape=jax.ShapeDtypeStruct((b, h, s, HEAD_W), F32),
        scratch_shapes=[
            pltpu.VMEM((s // ck, ck, 2 * tq), F32), pltpu.VMEM((c, 2 * tq), F32),
            pltpu.VMEM((s // ck, ck, 2 * tq), BF16), pltpu.VMEM((c, 2 * tq), BF16),
            m_buf, m_buf, f_buf, f_buf, r_buf, r_buf,
        ],
        compiler_params=pltpu.CompilerParams(
            dimension_semantics=("parallel", "parallel"), vmem_limit_bytes=VMEM_LIMIT),
        name="diff_attn",
    )(dl, subln, qk, qk, k_ctx, vt, vt_ctx)


def _out_kernel(u_ref, g2_ref, up_ref, un_ref, sig_ref, szb_ref, o_ref, x_ref, mod_ref, cw_ref,
                woa_ref, wob_ref, wo_ref, fw_ref, out_ref, *, tm):
    i = pl.program_id(1)
    last = pl.num_programs(1) - 1
    d = x_ref.shape[2]

    u = u_ref[0].astype(F32)
    u_prev = jnp.where(i > 0, up_ref[0, BF16_ROWS - 1:BF16_ROWS, :].astype(F32), 0.0)
    u_next = jnp.where(i < last, un_ref[0, 0:1, :].astype(F32), 0.0)
    row = lax.broadcasted_iota(jnp.int32, (tm, 1), 0)
    u_dn = jnp.where(row == 0, u_prev, pltpu.roll(u, 1, 0))
    u_up = jnp.where(row == tm - 1, u_next, pltpu.roll(u, tm - 1, 0))
    y = cw_ref[0:1, :] * u_dn + cw_ref[1:2, :] * u + cw_ref[2:3, :] * u_up
    a_in = (g2_ref[0].astype(F32) * y).astype(BF16)
    y_a = jnp.dot(a_in, woa_ref[...], preferred_element_type=F32)

    o = jnp.concatenate([o_ref[0, hh] for hh in range(N_HEADS)], axis=1)
    b_in = (o * szb_ref[0].astype(F32)).astype(BF16)
    y_b = jnp.dot(b_in, wob_ref[...], preferred_element_type=F32)

    merged = (sig_ref[0, :, 0:d].astype(F32) * y_a
              + sig_ref[0, :, d:2 * d].astype(F32) * y_b).astype(BF16)
    y_o = jnp.dot(merged, wo_ref[...], preferred_element_type=F32)

    xo = x_ref[0] + mod_ref[0, 2:3, :] * y_o
    out_ref[0] = xo * lax.rsqrt(jnp.mean(xo * xo, axis=-1, keepdims=True) + RMS_EPS) * fw_ref[...]


def _out_stage(ug, gz, o, x, mod, conv_w, woa, wob, wo, fw):
    b, s, d = x.shape
    cw = conv_w.shape[-1]
    tm = min(OUT_TILE, s)
    tb = tm // BF16_ROWS
    nblk = s // BF16_ROWS
    col = lambda cb: (lambda bi, i: (bi, i, cb))
    const2 = lambda bi, i: (0, 0)
    resident = dict(pipeline_mode=pl.Buffered(1))
    return pl.pallas_call(
        functools.partial(_out_kernel, tm=tm),
        grid=(b, s // tm),
        in_specs=[
            pl.BlockSpec((1, tm, cw), col(0)),
            pl.BlockSpec((1, tm, cw), col(1)),
            pl.BlockSpec((1, BF16_ROWS, cw), lambda bi, i: (bi, jnp.maximum(i * tb - 1, 0), 0)),
            pl.BlockSpec((1, BF16_ROWS, cw),
                         lambda bi, i: (bi, jnp.minimum((i + 1) * tb, nblk - 1), 0)),
            pl.BlockSpec((1, tm, 2 * d), col(0)),
            pl.BlockSpec((1, tm, COL_TILE), col(2 * d // COL_TILE)),
            pl.BlockSpec((1, N_HEADS, tm, HEAD_W), lambda bi, i: (bi, 0, i, 0)),
            pl.BlockSpec((1, tm, d), col(0)),
            pl.BlockSpec((1, 3, d), lambda bi, i: (bi, 0, 0)),
            pl.BlockSpec(conv_w.shape, const2),
            pl.BlockSpec(woa.shape, const2, **resident),
            pl.BlockSpec(wob.shape, const2, **resident),
            pl.BlockSpec(wo.shape, const2, **resident),
            pl.BlockSpec(fw.shape, const2),
        ],
        out_specs=pl.BlockSpec((1, tm, d), col(0)),
        out_shape=jax.ShapeDtypeStruct((b, s, d), x.dtype),
        compiler_params=pltpu.CompilerParams(
            dimension_semantics=("parallel", "parallel"), vmem_limit_bytes=VMEM_LIMIT),
        name="out_stage",
    )(ug, ug, ug, ug, gz, gz, o, x, mod, conv_w, woa, wob, wo, fw)


def _head_lane_order(w):
    lead = w.shape[:-1]
    w = w.reshape(lead + (N_HEADS, 2, 2, 2, AXIS_ROT // 2))
    n = len(lead)
    w = jnp.transpose(w, tuple(range(n)) + (n, n + 3, n + 1, n + 2, n + 4))
    return w.reshape(lead + (N_HEADS * HEAD_W,))


def _rope_tables(n_tokens):
    rows = n_tokens // GRID_W
    row = jnp.repeat(jnp.arange(rows, dtype=F32), GRID_W)
    col = jnp.tile(jnp.arange(GRID_W, dtype=F32), rows)
    inv_freq = ROPE_THETA ** (-jnp.arange(0, AXIS_ROT, 2, dtype=F32) / AXIS_ROT)
    ang = jnp.concatenate([row[:, None] * inv_freq, col[:, None] * inv_freq], axis=-1)
    ang = jnp.tile(ang, (1, 4))
    sign = jnp.where(jnp.arange(HEAD_W) < HEAD_W // 2, -1.0, 1.0).astype(F32)
    return jnp.cos(ang), jnp.sin(ang) * sign


def kernel(x, c, ctx, c_ctx, w_ada, b_ada, w_in, conv_w, diff_lambda, subln_w,
           w_out_a, w_out_b, w_o, final_norm_w):
    b, s, d = x.shape
    cw = conv_w.shape[-1]
    aw = w_out_b.shape[1]
    assert w_in.shape[0] == 1, "single-layer block"
    assert cw == COL_TILE and aw == COL_TILE == N_HEADS * HEAD_W and d == 2 * COL_TILE
    assert s % GRID_W == 0 and s % min(ROW_TILE, s) == 0 and ctx.shape[1] % LANES == 0
    lam_init = 0.8 - 0.6 * math.exp(-0.3 * 0)

    wi = w_in[0]
    q_off = 4 * cw
    wq = wi[:, q_off:q_off + aw]
    wk = wi[:, q_off + aw:q_off + 2 * aw]
    w = jnp.concatenate(
        [wi[:, :q_off], _head_lane_order(wq), _head_lane_order(wk), wi[:, q_off + 2 * aw:]],
        axis=1).astype(BF16)

    cc = jnp.concatenate([c, c_ctx[None, :]], axis=0)
    mod = _modulation(cc, w_ada[0], b_ada[0]).reshape(b + 1, 3, d)

    cos_t, sin_t = _rope_tables(s)
    ug, gz, qk, vt = _inproj(x, mod, w, cos_t, sin_t)
    k_ctx, vt_ctx = _ctx_proj(ctx, mod, w, (q_off + aw) // COL_TILE)

    o = _attention(diff_lambda[0], subln_w[0].reshape(HEAD_W, 1), qk, vt, k_ctx, vt_ctx, lam_init)

    return _out_stage(ug, gz, o, x, mod, conv_w[0],
                      w_out_a[0].astype(BF16), w_out_b[0].astype(BF16), w_o[0].astype(BF16),
                      final_norm_w.reshape(1, d))
```

```python
import functools
import math

import jax
import jax.numpy as jnp
from jax import lax
from jax.experimental import pallas as pl
from jax.experimental.pallas import tpu as pltpu

F32 = jnp.float32
BF16 = jnp.bfloat16

N_HEADS = 8
HEAD_DIM = 64
HEAD_W = 2 * HEAD_DIM
GRID_W = 64
ROPE_THETA = 10000.0
AXIS_ROT = HEAD_DIM // 2
RMS_EPS = 1e-6
QK_SCALE = HEAD_DIM ** -0.5
LOG2E = math.log2(math.e)
LANES = 128
SUBLANES = 8
BF16_ROWS = 16
COL_TILE = 1024

ROW_TILE = 1024
Q_TILE = 256
KEY_CHUNK = 512
HEAD_GROUP = 2
OUT_TILE = 512
VMEM_LIMIT = 56 * 1024 * 1024


def _silu(x):
    return x * jax.nn.sigmoid(x)


def _rms_mod(x, shift, scale):
    ms = jnp.mean(x * x, axis=-1, keepdims=True)
    return (x * lax.rsqrt(ms + RMS_EPS)) * (1.0 + scale) + shift


def _head_cols(t, hh):
    return t[:, hh * HEAD_W:(hh + 1) * HEAD_W]


def _mod_kernel(c_ref, w_ref, b_ref, o_ref):
    a = _silu(c_ref[...]).astype(BF16)
    o_ref[...] = jnp.dot(a, w_ref[...].astype(BF16), preferred_element_type=F32) + b_ref[...]


def _modulation(cc, w_ada, b_ada):
    n, d = cc.shape
    cols = w_ada.shape[1]
    return pl.pallas_call(
        _mod_kernel,
        grid=(cols // COL_TILE,),
        in_specs=[
            pl.BlockSpec((n, d), lambda j: (0, 0)),
            pl.BlockSpec((d, COL_TILE), lambda j: (0, j)),
            pl.BlockSpec((1, COL_TILE), lambda j: (0, j)),
        ],
        out_specs=pl.BlockSpec((n, COL_TILE), lambda j: (0, j)),
        out_shape=jax.ShapeDtypeStruct((n, cols), F32),
        compiler_params=pltpu.CompilerParams(
            dimension_semantics=("parallel",), vmem_limit_bytes=VMEM_LIMIT),
        name="adaln_mod",
    )(cc, w_ada, b_ada.reshape(1, cols))


def _rope_heads(ref, t, cos, sin, scale):
    for hh in range(N_HEADS):
        th = _head_cols(t, hh)
        rot = pltpu.roll(th, HEAD_W // 2, 1)
        ref[0, 0, hh] = ((th * cos + rot * sin) * scale).astype(ref.dtype)


def _store_heads_transposed(ref, t):
    for hh in range(N_HEADS):
        ref[0, hh] = _head_cols(t, hh).T.astype(ref.dtype)


def _norm_to_scratch(x_ref, mod_ref, h_ref):
    shift = mod_ref[0, 0:1, :]
    scale = mod_ref[0, 1:2, :]
    h_ref[...] = _rms_mod(x_ref[0], shift, scale).astype(BF16)


def _inproj_kernel(x_ref, mod_ref, w_ref, cos_ref, sin_ref,
                   ug_ref, gz_ref, qk_ref, vt_ref, h_ref, t_ref, *, n_gate):
    j = pl.program_id(2)
    first_gate = 4
    zb_step = first_gate + n_gate

    def project():
        return jnp.dot(h_ref[...], w_ref[...], preferred_element_type=F32)

    @pl.when(j == 0)
    def _():
        _norm_to_scratch(x_ref, mod_ref, h_ref)
        t_ref[...] = project()

    @pl.when(j == 1)
    def _():
        ug_ref[0] = (t_ref[...] * project()).astype(ug_ref.dtype)

    @pl.when((j == 2) | (j == first_gate))
    def _():
        t_ref[...] = project()

    @pl.when(j == 3)
    def _():
        ug_ref[0] = (project() * _silu(t_ref[...])).astype(ug_ref.dtype)

    @pl.when((j > first_gate) & (j <= zb_step))
    def _():
        gz_ref[0] = jax.nn.sigmoid(t_ref[...]).astype(gz_ref.dtype)
        t_ref[...] = project()

    @pl.when(j == zb_step + 1)
    def _():
        gz_ref[0] = _silu(t_ref[...]).astype(gz_ref.dtype)
        t_ref[...] = project()

    @pl.when(j == zb_step + 2)
    def _():
        _rope_heads(qk_ref, t_ref[...], cos_ref[...], sin_ref[...], QK_SCALE * LOG2E)
        t_ref[...] = project()

    @pl.when(j == zb_step + 3)
    def _():
        _rope_heads(qk_ref, t_ref[...], cos_ref[...], sin_ref[...], 1.0)
        _store_heads_transposed(vt_ref, project())


def _inproj_steps(n_gate):
    xa, gb, gc, za, q, k, v, zb = range(8)
    return [gc, xa, za, gb] + [8 + n for n in range(n_gate)] + [zb, q, k, v]


def _inproj(x, mod, w, cos_t, sin_t):
    b, s, d = x.shape
    tm = min(ROW_TILE, s)
    n_gate = 2 * d // COL_TILE
    steps = _inproj_steps(n_gate)
    n_steps = len(steps)

    def w_tile(bi, i, j):
        tile = steps[0]
        for n in range(1, n_steps):
            tile = jnp.where(j == n, steps[n], tile)
        return 0, tile

    return pl.pallas_call(
        functools.partial(_inproj_kernel, n_gate=n_gate),
        grid=(b, s // tm, n_steps),
        in_specs=[
            pl.BlockSpec((1, tm, d), lambda bi, i, j: (bi, i, 0)),
            pl.BlockSpec((1, 3, d), lambda bi, i, j: (bi, 0, 0)),
            pl.BlockSpec((d, COL_TILE), w_tile),
            pl.BlockSpec((tm, HEAD_W), lambda bi, i, j: (i, 0)),
            pl.BlockSpec((tm, HEAD_W), lambda bi, i, j: (i, 0)),
        ],
        out_specs=[
            pl.BlockSpec((1, tm, COL_TILE), lambda bi, i, j: (bi, i, jnp.where(j <= 1, 0, 1))),
            pl.BlockSpec((1, tm, COL_TILE), lambda bi, i, j: (bi, i, jnp.clip(j - 5, 0, n_gate))),
            pl.BlockSpec((1, 1, N_HEADS, tm, HEAD_W),
                         lambda bi, i, j: (bi, jnp.clip(j - (n_steps - 2), 0, 1), 0, i, 0)),
            pl.BlockSpec((1, N_HEADS, HEAD_W, tm), lambda bi, i, j: (bi, 0, 0, i)),
        ],
        out_shape=[
            jax.ShapeDtypeStruct((b, s, 2 * COL_TILE), BF16),
            jax.ShapeDtypeStruct((b, s, (n_gate + 1) * COL_TILE), BF16),
            jax.ShapeDtypeStruct((b, 2, N_HEADS, s, HEAD_W), BF16),
            jax.ShapeDtypeStruct((b, N_HEADS, HEAD_W, s), BF16),
        ],
        scratch_shapes=[pltpu.VMEM((tm, d), BF16), pltpu.VMEM((tm, COL_TILE), F32)],
        compiler_params=pltpu.CompilerParams(
            dimension_semantics=("parallel", "parallel", "arbitrary"),
            vmem_limit_bytes=VMEM_LIMIT),
        name="in_proj",
    )(x, mod, w, cos_t, sin_t)


def _ctx_kernel(x_ref, mod_ref, w_ref, k_ref, vt_ref, h_ref):
    j = pl.program_id(0)
    bi = pl.program_id(1)

    @pl.when(j == 0)
    def _():
        _norm_to_scratch(x_ref, mod_ref, h_ref.at[bi])
        t = jnp.dot(h_ref[bi], w_ref[...], preferred_element_type=F32)
        for hh in range(N_HEADS):
            k_ref[0, hh] = _head_cols(t, hh).astype(k_ref.dtype)

    @pl.when(j == 1)
    def _():
        _store_heads_transposed(
            vt_ref, jnp.dot(h_ref[bi], w_ref[...], preferred_element_type=F32))


def _ctx_proj(ctx, mod, w, k_col):
    b, c, d = ctx.shape
    ctx_row = mod.shape[0] - 1
    return pl.pallas_call(
        _ctx_kernel,
        grid=(2, b),
        in_specs=[
            pl.BlockSpec((1, c, d), lambda j, bi: (jnp.where(j == 0, bi, b - 1), 0, 0)),
            pl.BlockSpec((1, 3, d), lambda j, bi: (ctx_row, 0, 0)),
            pl.BlockSpec((d, COL_TILE), lambda j, bi: (0, k_col + j)),
        ],
        out_specs=[
            pl.BlockSpec((1, N_HEADS, c, HEAD_W),
                         lambda j, bi: (jnp.where(j == 0, bi, b - 1), 0, 0, 0)),
            pl.BlockSpec((1, N_HEADS, HEAD_W, c),
                         lambda j, bi: (jnp.where(j == 0, 0, bi), 0, 0, 0)),
        ],
        out_shape=[
            jax.ShapeDtypeStruct((b, N_HEADS, c, HEAD_W), BF16),
            jax.ShapeDtypeStruct((b, N_HEADS, HEAD_W, c), BF16),
        ],
        scratch_shapes=[pltpu.VMEM((b, c, d), BF16)],
        compiler_params=pltpu.CompilerParams(
            dimension_semantics=("arbitrary", "arbitrary"), vmem_limit_bytes=VMEM_LIMIT),
        name="ctx_proj",
    )(ctx, mod, w)


def _nt_dot(a, b):
    return lax.dot_general(a, b, (((1,), (1,)), ((), ())), preferred_element_type=F32)


def _fold_rows(x, op):
    return op(x.reshape(x.shape[0] // SUBLANES, SUBLANES, x.shape[1]), axis=0)


def _attn_kernel(dl_ref, sub_ref, q_ref, kl_ref, kc_ref, vtl_ref, vtc_ref, o_ref,
                 s_ref, sc_ref, p_ref, pc_ref, ma_ref, mb_ref, fa_ref, fb_ref, ra_ref, rb_ref,
                 *, tq, ck, lam_init):
    n_heads, s_len, _ = q_ref.shape
    n_tiles = s_len // tq
    tile_bits = n_tiles.bit_length() - 1
    n_steps = n_heads * n_tiles
    m_bufs = (ma_ref, mb_ref)
    f_bufs = ((fa_ref, ra_ref), (fb_ref, rb_ref))

    dl = dl_ref[...]
    lam = (jnp.exp(jnp.sum(dl[0:1] * dl[1:2], axis=1, keepdims=True))
           - jnp.exp(jnp.sum(dl[2:3] * dl[3:4], axis=1, keepdims=True)) + lam_init)
    sub = jnp.broadcast_to(sub_ref[...] * (1.0 - lam_init), (HEAD_W, tq))
    lane = lax.broadcasted_iota(jnp.int32, (1, HEAD_W), 1)
    comp0 = (lane % HEAD_DIM) < (HEAD_DIM // 2)

    def head_tile(t):
        if isinstance(t, int):
            return divmod(t, n_tiles)
        return lax.shift_right_logical(t, tile_bits), lax.bitwise_and(t, n_tiles - 1)

    def step(t_out, t_exp, t_qk, par):
        f_ref, r_ref = f_bufs[par]
        m_exp, m_qk = m_bufs[1 - par], m_bufs[par]
        if t_out is not None:
            h_out, i_out = head_tile(t_out)
        if t_qk is not None:
            h_qk, i_qk = head_tile(t_qk)
            q = q_ref[h_qk, pl.ds(pl.multiple_of(i_qk * tq, tq), tq), :]
            zero = jnp.zeros_like(q)
            lhs = jnp.concatenate([jnp.where(comp0, q, zero), jnp.where(comp0, zero, q)], axis=0)
        acc = m8 = l8 = None
        for c in range(-1, s_len // ck):
            rows = slice(None) if c < 0 else slice(c * ck, (c + 1) * ck)
            s_c, p_c = (sc_ref, pc_ref) if c < 0 else (s_ref.at[c], p_ref.at[c])
            if t_out is not None:
                vt = vtc_ref[h_out] if c < 0 else vtl_ref[h_out, :, rows]
                p = p_c[...]
                n = p.shape[0]
                p0 = p[:, 0:tq].reshape(n // BF16_ROWS, BF16_ROWS, tq)
                p1 = p[:, tq:2 * tq].reshape(n // BF16_ROWS, BF16_ROWS, tq)
                a = (p0 - f_ref[...] * p1).reshape(n, tq)
                part = jnp.dot(vt, a, preferred_element_type=F32)
                acc = part if acc is None else acc + part
            if t_exp is not None:
                s = s_c[...]
                n = s.shape[0]
                p = jnp.exp2(s.reshape(n // SUBLANES, SUBLANES, 2 * tq) - m_exp[...])
                p_c[...] = p.reshape(n, 2 * tq).astype(BF16)
                part = jnp.sum(p, axis=0)
                l8 = part if l8 is None else l8 + part
            if t_qk is not None:
                keys = kc_ref[h_qk] if c < 0 else kl_ref[h_qk, rows, :]
                s = _nt_dot(keys, lhs)
                s_c[...] = s
                part = _fold_rows(s, jnp.max)
                m8 = part if m8 is None else jnp.maximum(m8, part)
        if t_out is not None:
            o = (acc.reshape(HEAD_W // SUBLANES, SUBLANES, tq) * r_ref[...]).reshape(HEAD_W, tq)
            o = o * lax.rsqrt(jnp.mean(o * o, axis=0, keepdims=True) + RMS_EPS) * sub
            o_ref[0, h_out, pl.ds(pl.multiple_of(i_out * tq, tq), tq), :] = o.T.astype(o_ref.dtype)
        if t_exp is not None:
            l = jnp.sum(l8, axis=0, keepdims=True)
            l0, l1 = l[:, 0:tq], l[:, tq:2 * tq]
            f_next, r_next = f_bufs[1 - par]
            f_next[...] = jnp.broadcast_to(lam * l0 / l1, f_next.shape).astype(BF16)
            r_next[...] = jnp.broadcast_to(1.0 / l0, r_next.shape)
        if t_qk is not None:
            m_qk[...] = jnp.broadcast_to(jnp.max(m8, axis=0, keepdims=True), m_qk.shape)

    step(None, None, 0, 0)
    step(None, 0, 1, 1)

    def pair(u, carry):
        t = 2 * u
        step(t, t + 1, t + 2, 0)
        step(t + 1, t + 2, t + 3, 1)
        return carry

    lax.fori_loop(0, n_steps // 2 - 1, pair, 0)
    step(n_steps - 2, n_steps - 1, None, 0)
    step(n_steps - 1, None, None, 1)


def _attention(dl, subln, qk, vt, k_ctx, vt_ctx, lam_init):
    b, _, h, s, _ = qk.shape
    c = k_ctx.shape[2]
    hg = min(HEAD_GROUP, h)
    tq = min(Q_TILE, s // 2)
    ck = min(KEY_CHUNK, s)
    n_tiles = s // tq
    assert s % (2 * tq) == 0 and s % ck == 0 and h % hg == 0 and n_tiles & (n_tiles - 1) == 0
    m_buf = pltpu.VMEM((SUBLANES, 2 * tq), F32)
    f_buf = pltpu.VMEM((BF16_ROWS, tq), BF16)
    r_buf = pltpu.VMEM((SUBLANES, tq), F32)
    return pl.pallas_call(
        functools.partial(_attn_kernel, tq=tq, ck=ck, lam_init=lam_init),
        grid=(b, h // hg),
        in_specs=[
            pl.BlockSpec(dl.shape, lambda bi, gi: (0, 0)),
            pl.BlockSpec(subln.shape, lambda bi, gi: (0, 0)),
            pl.BlockSpec((None, None, hg, s, HEAD_W), lambda bi, gi: (bi, 0, gi, 0, 0)),
            pl.BlockSpec((None, None, hg, s, HEAD_W), lambda bi, gi: (bi, 1, gi, 0, 0)),
            pl.BlockSpec((None, hg, c, HEAD_W), lambda bi, gi: (bi, gi, 0, 0)),
            pl.BlockSpec((None, hg, HEAD_W, s), lambda bi, gi: (bi, gi, 0, 0)),
            pl.BlockSpec((None, hg, HEAD_W, c), lambda bi, gi: (bi, gi, 0, 0)),
        ],
        out_specs=pl.BlockSpec((1, hg, s, HEAD_W), lambda bi, gi: (bi, gi, 0, 0)),
        out_shape=jax.ShapeDtypeStruct((b, h, s, HEAD_W), F32),
        scratch_shapes=[
            pltpu.VMEM((s // ck, ck, 2 * tq), F32), pltpu.VMEM((c, 2 * tq), F32),
            pltpu.VMEM((s // ck, ck, 2 * tq), BF16), pltpu.VMEM((c, 2 * tq), BF16),
            m_buf, m_buf, f_buf, f_buf, r_buf, r_buf,
        ],
        compiler_params=pltpu.CompilerParams(
            dimension_semantics=("parallel", "parallel"), vmem_limit_bytes=VMEM_LIMIT),
        name="diff_attn",
    )(dl, subln, qk, qk, k_ctx, vt, vt_ctx)


def _out_kernel(u_ref, g2_ref, up_ref, un_ref, sig_ref, szb_ref, o_ref, x_ref, mod_ref, cw_ref,
                woa_ref, wob_ref, wo_ref, fw_ref, out_ref, *, tm):
    i = pl.program_id(1)
    last = pl.num_programs(1) - 1
    d = x_ref.shape[2]

    u = u_ref[0].astype(F32)
    u_prev = jnp.where(i > 0, up_ref[0, BF16_ROWS - 1:BF16_ROWS, :].astype(F32), 0.0)
    u_next = jnp.where(i < last, un_ref[0, 0:1, :].astype(F32), 0.0)
    row = lax.broadcasted_iota(jnp.int32, (tm, 1), 0)
    u_dn = jnp.where(row == 0, u_prev, pltpu.roll(u, 1, 0))
    u_up = jnp.where(row == tm - 1, u_next, pltpu.roll(u, tm - 1, 0))
    y = cw_ref[0:1, :] * u_dn + cw_ref[1:2, :] * u + cw_ref[2:3, :] * u_up
    a_in = (g2_ref[0].astype(F32) * y).astype(BF16)
    y_a = jnp.dot(a_in, woa_ref[...], preferred_element_type=F32)

    o = jnp.concatenate([o_ref[0, hh] for hh in range(N_HEADS)], axis=1)
    b_in = (o * szb_ref[0].astype(F32)).astype(BF16)
    y_b = jnp.dot(b_in, wob_ref[...], preferred_element_type=F32)

    merged = (sig_ref[0, :, 0:d].astype(F32) * y_a
              + sig_ref[0, :, d:2 * d].astype(F32) * y_b).astype(BF16)
    y_o = jnp.dot(merged, wo_ref[...], preferred_element_type=F32)

    xo = x_ref[0] + mod_ref[0, 2:3, :] * y_o
    out_ref[0] = xo * lax.rsqrt(jnp.mean(xo * xo, axis=-1, keepdims=True) + RMS_EPS) * fw_ref[...]


def _out_stage(ug, gz, o, x, mod, conv_w, woa, wob, wo, fw):
    b, s, d = x.shape
    cw = conv_w.shape[-1]
    tm = min(OUT_TILE, s)
    tb = tm // BF16_ROWS
    nblk = s // BF16_ROWS
    col = lambda cb: (lambda bi, i: (bi, i, cb))
    const2 = lambda bi, i: (0, 0)
    resident = dict(pipeline_mode=pl.Buffered(1))
    return pl.pallas_call(
        functools.partial(_out_kernel, tm=tm),
        grid=(b, s // tm),
        in_specs=[
            pl.BlockSpec((1, tm, cw), col(0)),
            pl.BlockSpec((1, tm, cw), col(1)),
            pl.BlockSpec((1, BF16_ROWS, cw), lambda bi, i: (bi, jnp.maximum(i * tb - 1, 0), 0)),
            pl.BlockSpec((1, BF16_ROWS, cw),
                         lambda bi, i: (bi, jnp.minimum((i + 1) * tb, nblk - 1), 0)),
            pl.BlockSpec((1, tm, 2 * d), col(0)),
            pl.BlockSpec((1, tm, COL_TILE), col(2 * d // COL_TILE)),
            pl.BlockSpec((1, N_HEADS, tm, HEAD_W), lambda bi, i: (bi, 0, i, 0)),
            pl.BlockSpec((1, tm, d), col(0)),
            pl.BlockSpec((1, 3, d), lambda bi, i: (bi, 0, 0)),
            pl.BlockSpec(conv_w.shape, const2),
            pl.BlockSpec(woa.shape, const2, **resident),
            pl.BlockSpec(wob.shape, const2, **resident),
            pl.BlockSpec(wo.shape, const2, **resident),
            pl.BlockSpec(fw.shape, const2),
        ],
        out_specs=pl.BlockSpec((1, tm, d), col(0)),
        out_shape=jax.ShapeDtypeStruct((b, s, d), x.dtype),
        compiler_params=pltpu.CompilerParams(
            dimension_semantics=("parallel", "parallel"), vmem_limit_bytes=VMEM_LIMIT),
        name="out_stage",
    )(ug, ug, ug, ug, gz, gz, o, x, mod, conv_w, woa, wob, wo, fw)


def _head_lane_order(w):
    lead = w.shape[:-1]
    w = w.reshape(lead + (N_HEADS, 2, 2, 2, AXIS_ROT // 2))
    n = len(lead)
    w = jnp.transpose(w, tuple(range(n)) + (n, n + 3, n + 1, n + 2, n + 4))
    return w.reshape(lead + (N_HEADS * HEAD_W,))


def _rope_tables(n_tokens):
    rows = n_tokens // GRID_W
    row = jnp.repeat(jnp.arange(rows, dtype=F32), GRID_W)
    col = jnp.tile(jnp.arange(GRID_W, dtype=F32), rows)
    inv_freq = ROPE_THETA ** (-jnp.arange(0, AXIS_ROT, 2, dtype=F32) / AXIS_ROT)
    ang = jnp.concatenate([row[:, None] * inv_freq, col[:, None] * inv_freq], axis=-1)
    ang = jnp.tile(ang, (1, 4))
    sign = jnp.where(jnp.arange(HEAD_W) < HEAD_W // 2, -1.0, 1.0).astype(F32)
    return jnp.cos(ang), jnp.sin(ang) * sign


def kernel(x, c, ctx, c_ctx, w_ada, b_ada, w_in, conv_w, diff_lambda, subln_w,
           w_out_a, w_out_b, w_o, final_norm_w):
    b, s, d = x.shape
    cw = conv_w.shape[-1]
    aw = w_out_b.shape[1]
    assert w_in.shape[0] == 1, "single-layer block"
    assert cw == COL_TILE and aw == COL_TILE == N_HEADS * HEAD_W and d == 2 * COL_TILE
    assert s % GRID_W == 0 and s % min(ROW_TILE, s) == 0 and ctx.shape[1] % LANES == 0
    lam_init = 0.8 - 0.6 * math.exp(-0.3 * 0)

    wi = w_in[0]
    q_off = 4 * cw
    wqk = _head_lane_order(wi[:, q_off:q_off + 2 * aw].reshape(d, 2, aw)).reshape(d, 2 * aw)
    w = lax.dynamic_update_slice(wi.astype(BF16), wqk.astype(BF16), (0, q_off))

    cc = jnp.concatenate([c, c_ctx[None, :]], axis=0)
    mod = _modulation(cc, w_ada[0], b_ada[0]).reshape(b + 1, 3, d)

    cos_t, sin_t = _rope_tables(s)
    ug, gz, qk, vt = _inproj(x, mod, w, cos_t, sin_t)
    k_ctx, vt_ctx = _ctx_proj(ctx, mod, w, (q_off + aw) // COL_TILE)

    o = _attention(diff_lambda[0], subln_w[0].reshape(HEAD_W, 1), qk, vt, k_ctx, vt_ctx, lam_init)

    return _out_stage(ug, gz, o, x, mod, conv_w[0],
                      w_out_a[0].astype(BF16), w_out_b[0].astype(BF16), w_o[0].astype(BF16),
                      final_norm_w.reshape(1, d))
```
